```python
import math
import jax, jax.numpy as jnp
from jax import lax
import numpy as np

D_MODEL = 4096
BATCH = 4
SEQ = 4096
DEPTH = 1

SB_HEAD_DIM = 128
SB_WIDTH = D_MODEL // 2
SB_HEADS = SB_WIDTH // SB_HEAD_DIM
Q_BLOCK = 128
POOL_SIZES = (2, 4, 8, 16)
POOL_GROUPS = len(POOL_SIZES)
POOL_WIDTH = D_MODEL // 2
POOL_GROUP_DIM = POOL_WIDTH // POOL_GROUPS
N_BRANCHES = 2
IN_COLS = 3 * SB_WIDTH + POOL_WIDTH + N_BRANCHES * D_MODEL
N_GROUPS = 4
EXPERTS_PER_GROUP = 8
N_EXPERTS = N_GROUPS * EXPERTS_PER_GROUP
TOP_K = 2
D_EXPERT = D_MODEL // 4
MOE_BLOCK = 128
RMS_EPS = 1e-6

kernel_name = "hybrid_stickbreak_pool_hmoe_block"


def rmsnorm(x, g):
    xf = x.astype(jnp.float32)
    y = xf * lax.rsqrt(jnp.mean(xf * xf, axis=-1, keepdims=True) + RMS_EPS)
    return (y * g.astype(jnp.float32)).astype(x.dtype)


def stick_breaking_attention(q, k, v):
    B, H, S, Dh = q.shape
    nb = S // Q_BLOCK
    scale = 1.0 / math.sqrt(Dh)
    kf = k.astype(jnp.float32)
    vf = v.astype(jnp.float32)
    key_pos = jnp.arange(S)
    qb = q.reshape(B, H, nb, Q_BLOCK, Dh).transpose(2, 0, 1, 3, 4)

    def block(args):
        q_blk, i = args
        z = jnp.einsum('bhqd,bhkd->bhqk', q_blk.astype(jnp.float32), kf) * scale
        q_pos = i * Q_BLOCK + jnp.arange(Q_BLOCK)
        mask = key_pos[None, :] < q_pos[:, None]
        log_not_beta = jnp.where(mask, -jax.nn.softplus(z), 0.0)
        after = lax.cumsum(log_not_beta, axis=3, reverse=True) - log_not_beta
        w = jnp.where(mask, jnp.exp(jax.nn.log_sigmoid(z) + after), 0.0)
        return jnp.einsum('bhqk,bhkd->bhqd', w, vf)

    out = lax.map(block, (qb, jnp.arange(nb)))
    return out.transpose(1, 2, 0, 3, 4).reshape(B, H, S, Dh).astype(q.dtype)


def causal_multiscale_pool(u):
    B, S, _ = u.shape
    uf = u.astype(jnp.float32).reshape(B, S, POOL_GROUPS, POOL_GROUP_DIM)
    c = jnp.cumsum(uf, axis=1)
    c = jnp.concatenate([jnp.zeros_like(c[:, :1]), c], axis=1)
    pos = jnp.arange(S)
    windows = jnp.array(POOL_SIZES, dtype=jnp.int32)
    start = jnp.maximum(pos[:, None] + 1 - windows[None, :], 0)
    g_idx = jnp.arange(POOL_GROUPS)[None, :]
    window_sum = c[:, 1:] - c[:, start, g_idx]
    count = (pos[:, None] + 1 - start).astype(jnp.float32)
    return window_sum / count[None, :, :, None] - uf


def hierarchical_moe(h, w_router_group, b_router_group, w_router_expert, b_router_expert,
                     w_gate, w_up, w_down):
    B, S, D = h.shape
    N = B * S
    hf = h.reshape(N, D)
    grp_logits = (hf @ w_router_group).astype(jnp.float32) + b_router_group.astype(jnp.float32)
    grp_prob = jax.nn.softmax(grp_logits, axis=-1)
    g_top = jnp.argmax(grp_logits, axis=-1)
    p_group = jnp.take_along_axis(grp_prob, g_top[:, None], axis=-1)
    exp_logits = ((hf @ w_router_expert).astype(jnp.float32)
                  + b_router_expert.astype(jnp.float32)).reshape(N, N_GROUPS, EXPERTS_PER_GROUP)
    in_group = jnp.take_along_axis(exp_logits, g_top[:, None, None], axis=1)[:, 0]
    top_val, top_idx = lax.top_k(in_group, TOP_K)
    weights = p_group * jax.nn.softmax(top_val, axis=-1)
    expert_ids = g_top[:, None] * EXPERTS_PER_GROUP + top_idx

    A = N * TOP_K
    flat_e = expert_ids.reshape(A)
    flat_tok = jnp.arange(A, dtype=jnp.int32) // TOP_K
    flat_w = weights.reshape(A)
    order = jnp.argsort(flat_e)
    sorted_e = flat_e[order]
    counts = jnp.bincount(flat_e, length=N_EXPERTS)
    starts = jnp.cumsum(counts) - counts
    padded = (counts + MOE_BLOCK - 1) // MOE_BLOCK * MOE_BLOCK
    pends = jnp.cumsum(padded)
    pstarts = pends - padded
    dest = pstarts[sorted_e] + (jnp.arange(A) - starts[sorted_e])
    n_blocks = -(-A // MOE_BLOCK) + N_EXPERTS
    rows = n_blocks * MOE_BLOCK
    row_tok = jnp.zeros((rows,), jnp.int32).at[dest].set(flat_tok[order])
    row_w = jnp.zeros((rows,), jnp.float32).at[dest].set(flat_w[order])
    block_start = jnp.arange(n_blocks) * MOE_BLOCK
    block_e = jnp.clip(jnp.searchsorted(pends, block_start, side='right'), 0, N_EXPERTS - 1)
    xs = hf[row_tok].reshape(n_blocks, MOE_BLOCK, D)

    def expert_block(args):
        xb, e = args
        return (jax.nn.silu(xb @ w_gate[e]) * (xb @ w_up[e])) @ w_down[e]

    ys = lax.map(expert_block, (xs, block_e)).reshape(rows, D)
    out = jnp.zeros((N, D), jnp.float32).at[row_tok].add(ys.astype(jnp.float32) * row_w[:, None])
    return out.astype(h.dtype).reshape(B, S, D)


def setup_inputs(seed: int = 0) -> dict:
    key = jax.random.key(seed)
    ks = jax.random.split(key, 20)
    f32 = jnp.float32
    nrm = lambda k, shape, fan_in: jax.random.normal(k, shape, f32) * (fan_in ** -0.5)
    gain = lambda k, shape: 1.0 + 0.02 * jax.random.normal(k, shape, f32)
    return {
        "x": jax.random.normal(ks[0], (BATCH, SEQ, D_MODEL), f32),
        "norm_mix": gain(ks[1], (DEPTH, D_MODEL)),
        "w_in": nrm(ks[2], (DEPTH, D_MODEL, IN_COLS), D_MODEL),
        "w_o_attn": nrm(ks[3], (DEPTH, SB_WIDTH, D_MODEL), SB_WIDTH),
        "pool_w": nrm(ks[4], (DEPTH, POOL_GROUPS, POOL_GROUP_DIM, POOL_GROUP_DIM), POOL_GROUP_DIM),
        "pool_scale": gain(ks[5], (DEPTH, POOL_GROUPS, POOL_GROUP_DIM)),
        "w_o_pool": nrm(ks[6], (DEPTH, POOL_WIDTH, D_MODEL), POOL_WIDTH),
        "w_out": nrm(ks[7], (DEPTH, D_MODEL, D_MODEL), D_MODEL),
        "norm_ffn": gain(ks[8], (DEPTH, D_MODEL)),
        "w_router_group": nrm(ks[9], (DEPTH, D_MODEL, N_GROUPS), D_MODEL),
        "b_router_group": 0.01 * jax.random.normal(ks[10], (DEPTH, N_GROUPS), f32),
        "w_router_expert": nrm(ks[11], (DEPTH, D_MODEL, N_EXPERTS), D_MODEL),
        "b_router_expert": 0.01 * jax.random.normal(ks[12], (DEPTH, N_EXPERTS), f32),
        "w_gate": nrm(ks[13], (DEPTH, N_EXPERTS, D_MODEL, D_EXPERT), D_MODEL),
        "w_up": nrm(ks[14], (DEPTH, N_EXPERTS, D_MODEL, D_EXPERT), D_MODEL),
        "w_down": nrm(ks[15], (DEPTH, N_EXPERTS, D_EXPERT, D_MODEL), D_EXPERT),
        "norm_final": gain(ks[16], (D_MODEL,)),
    }


def reference(x, norm_mix, w_in, w_o_attn, pool_w, pool_scale, w_o_pool, w_out, norm_ffn,
              w_router_group, b_router_group, w_router_expert, b_router_expert,
              w_gate, w_up, w_down, norm_final):
    B, S, D = x.shape
    for l in range(DEPTH):
        h = rmsnorm(x, norm_mix[l])
        proj = h @ w_in[l]
        q, k, v, u, gate_logits = jnp.split(
            proj, [SB_WIDTH, 2 * SB_WIDTH, 3 * SB_WIDTH, 3 * SB_WIDTH + POOL_WIDTH], axis=-1)
        to_heads = lambda t: t.reshape(B, S, SB_HEADS, SB_HEAD_DIM).transpose(0, 2, 1, 3)
        o_attn = stick_breaking_attention(to_heads(q), to_heads(k), to_heads(v))
        y_attn = o_attn.transpose(0, 2, 1, 3).reshape(B, S, SB_WIDTH) @ w_o_attn[l]
        pooled = causal_multiscale_pool(u)
        p_mixed = jnp.einsum('bsgc,gcd->bsgd', pooled, pool_w[l].astype(jnp.float32)) \
            * pool_scale[l].astype(jnp.float32)
        y_pool = p_mixed.reshape(B, S, POOL_WIDTH).astype(x.dtype) @ w_o_pool[l]
        g_attn, g_pool = jnp.split(jax.nn.sigmoid(gate_logits), 2, axis=-1)
        x = x + (g_attn * y_attn + g_pool * y_pool) @ w_out[l]
        h2 = rmsnorm(x, norm_ffn[l])
        x = x + hierarchical_moe(h2, w_router_group[l], b_router_group[l], w_router_expert[l],
                                 b_router_expert[l], w_gate[l], w_up[l], w_down[l])
    return rmsnorm(x, norm_final)
```

```python
import functools
import math

import jax
import jax.numpy as jnp
from jax import lax
from jax.experimental import pallas as pl
from jax.experimental.pallas import tpu as pltpu

F32 = jnp.float32
BF16 = jnp.bfloat16
U32 = jnp.uint32
I32 = jnp.int32

HEAD_DIM = 128
POOL_SIZES = (2, 4, 8, 16)
POOL_HALO = 16
TOP_K = 2
RMS_EPS = 1e-6
LANES = 128
VMEM_LIMIT = 56 * 1024 * 1024
NEG_BIG = -1e30


def _cparams(sem):
    return pltpu.CompilerParams(dimension_semantics=sem, vmem_limit_bytes=VMEM_LIMIT)


def _tile(pref, dim):
    t = min(pref, dim)
    assert dim % t == 0, (pref, dim)
    return t


def _rmsnorm_kernel(x_ref, g_ref, o_ref):
    x = x_ref[...]
    ms = jnp.mean(x * x, axis=-1, keepdims=True)
    o_ref[...] = (x * lax.rsqrt(ms + RMS_EPS) * g_ref[...]).astype(o_ref.dtype)


def _rmsnorm(x, g, out_dtype):
    n, d = x.shape
    tm = _tile(256, n)
    return pl.pallas_call(
        _rmsnorm_kernel,
        grid=(n // tm,),
        in_specs=[pl.BlockSpec((tm, d), lambda i: (i, 0)),
                  pl.BlockSpec((1, d), lambda i: (0, 0))],
        out_specs=pl.BlockSpec((tm, d), lambda i: (i, 0)),
        out_shape=jax.ShapeDtypeStruct((n, d), out_dtype),
        compiler_params=_cparams(("parallel",)),
        name="rmsnorm",
    )(x, g.reshape(1, d))


def _mm_kernel(a_ref, b_ref, o_ref):
    o_ref[...] = jnp.dot(a_ref[...], b_ref[...], preferred_element_type=F32).astype(o_ref.dtype)


def _mm_res_kernel(a_ref, b_ref, r_ref, o_ref):
    o_ref[...] = (r_ref[...] + jnp.dot(a_ref[...], b_ref[...], preferred_element_type=F32)
                  ).astype(o_ref.dtype)


def _matmul(a, b, out_dtype, *, tm, tn, residual=None, name):
    m, k = a.shape
    _, n = b.shape
    tm, tn = _tile(tm, m), _tile(tn, n)
    in_specs = [pl.BlockSpec((tm, k), lambda i, j: (i, 0)),
                pl.BlockSpec((k, tn), lambda i, j: (0, j))]
    args = [a, b]
    body = _mm_kernel
    if residual is not None:
        in_specs.append(pl.BlockSpec((tm, tn), lambda i, j: (i, j)))
        args.append(residual)
        body = _mm_res_kernel
    return pl.pallas_call(
        body,
        grid=(m // tm, n // tn),
        in_specs=in_specs,
        out_specs=pl.BlockSpec((tm, tn), lambda i, j: (i, j)),
        out_shape=jax.ShapeDtypeStruct((m, n), out_dtype),
        compiler_params=_cparams(("parallel", "parallel")),
        name=name,
    )(*args)


def _attn_kernel(q_ref, k_ref, v_ref, o_ref, *, tq, tk, scale):
    i = pl.program_id(2)
    row = lax.broadcasted_iota(I32, (tk, 2 * tk), 0)
    col = lax.broadcasted_iota(I32, (tk, 2 * tk), 1)
    tri = jnp.where((row >= col) | (col >= tk), 1.0, 0.0).astype(BF16)
    qi = lax.broadcasted_iota(I32, (tk, tk), 0)
    ki = lax.broadcasted_iota(I32, (tk, tk), 1)
    causal = ki < qi

    def tile(q, j, carry, acc, masked):
        start = pl.multiple_of(j * tk, tk)
        kj = k_ref[pl.ds(start, tk), :]
        vj = v_ref[pl.ds(start, tk), :]
        z = lax.dot_general(q, kj, (((1,), (1,)), ((), ())), preferred_element_type=F32) * scale
        sp = jnp.maximum(z, 0.0) + jnp.log(1.0 + jnp.exp(-jnp.abs(z)))
        if masked:
            sp = jnp.where(causal, sp, 0.0)
        cs = jnp.dot(sp.astype(BF16), tri, preferred_element_type=F32)
        w = jnp.exp(z - cs[:, :tk] - carry)
        if masked:
            w = jnp.where(causal, w, 0.0)
        acc = acc + jnp.dot(w.astype(BF16), vj, preferred_element_type=F32)
        return carry + cs[:, tk:], acc

    for r in range(tq // tk):
        q = q_ref[r * tk:(r + 1) * tk, :]
        jd = i * (tq // tk) + r
        zero = jnp.zeros((tk, tk), F32)
        carry, acc = tile(q, jd, zero, jnp.zeros((tk, HEAD_DIM), F32), True)

        def body(n, c, q=q, jd=jd):
            return tile(q, jd - 1 - n, c[0], c[1], False)

        carry, acc = lax.fori_loop(0, jd, body, (carry, acc))
        o_ref[r * tk:(r + 1) * tk, :] = acc.astype(o_ref.dtype)


def _attention(proj, batch, seq, n_heads):
    n = batch * seq
    tq = _tile(256, seq)
    tk = HEAD_DIM
    nq = seq // tq
    kern = functools.partial(_attn_kernel, tq=tq, tk=tk, scale=1.0 / math.sqrt(HEAD_DIM))
    return pl.pallas_call(
        kern,
        grid=(batch, n_heads, nq),
        in_specs=[pl.BlockSpec((tq, HEAD_DIM), lambda b, h, i: (b * nq + i, h)),
                  pl.BlockSpec((seq, HEAD_DIM), lambda b, h, i: (b, n_heads + h)),
                  pl.BlockSpec((seq, HEAD_DIM), lambda b, h, i: (b, 2 * n_heads + h))],
        out_specs=pl.BlockSpec((tq, HEAD_DIM), lambda b, h, i: (b * nq + i, h)),
        out_shape=jax.ShapeDtypeStruct((n, n_heads * HEAD_DIM), BF16),
        compiler_params=_cparams(("parallel", "parallel", "arbitrary")),
        name="stickbreak_attn",
    )(proj, proj, proj)


def _pool_kernel(u_ref, halo_ref, w_ref, s_ref, o_ref, buf, *, ts, gd):
    i = pl.program_id(1)
    pos = i * ts + lax.broadcasted_iota(I32, (ts, gd), 0)
    for g, win in enumerate(POOL_SIZES):
        cols = slice(g * gd, (g + 1) * gd)
        u = u_ref[:, cols].astype(F32)
        halo = halo_ref[:, cols].astype(F32)
        buf[0:POOL_HALO, :] = jnp.where(i > 0, halo, 0.0)
        buf[POOL_HALO:POOL_HALO + ts, :] = u
        acc = u
        for k in range(1, win):
            acc = acc + buf[POOL_HALO - k:POOL_HALO - k + ts, :]
        cnt = jnp.minimum(pos + 1, win).astype(F32)
        pooled = acc / cnt - u
        mixed = jnp.dot(pooled.astype(BF16), w_ref[g], preferred_element_type=F32) * s_ref[g]
        o_ref[:, cols] = mixed.astype(o_ref.dtype)


def _pool_mix(proj, pool_w, pool_scale, batch, seq, col_block):
    n = batch * seq
    groups, gd, _ = pool_w.shape
    width = groups * gd
    ts = _tile(512, seq)
    ns = seq // ts
    hb = ts // POOL_HALO
    kern = functools.partial(_pool_kernel, ts=ts, gd=gd)
    return pl.pallas_call(
        kern,
        grid=(batch, ns),
        in_specs=[pl.BlockSpec((ts, width), lambda b, i: (b * ns + i, col_block)),
                  pl.BlockSpec((POOL_HALO, width),
                               lambda b, i: (jnp.maximum((b * ns + i) * hb - 1, 0), col_block)),
                  pl.BlockSpec((groups, gd, gd), lambda b, i: (0, 0, 0)),
                  pl.BlockSpec((groups, 1, gd), lambda b, i: (0, 0, 0))],
        out_specs=pl.BlockSpec((ts, width), lambda b, i: (b * ns + i, 0)),
        out_shape=jax.ShapeDtypeStruct((n, width), BF16),
        scratch_shapes=[pltpu.VMEM((POOL_HALO + ts, gd), F32)],
        compiler_params=_cparams(("parallel", "arbitrary")),
        name="pool_mix",
    )(proj, proj, pool_w, pool_scale.reshape(groups, 1, gd))


def _gated_kernel(oa_ref, pm_ref, wa_ref, wp_ref, ga_ref, gp_ref, o_ref):
    ya = jnp.dot(oa_ref[...], wa_ref[...], preferred_element_type=F32)
    yp = jnp.dot(pm_ref[...], wp_ref[...], preferred_element_type=F32)
    ga = jax.nn.sigmoid(ga_ref[...].astype(F32))
    gp = jax.nn.sigmoid(gp_ref[...].astype(F32))
    o_ref[...] = (ga * ya + gp * yp).astype(o_ref.dtype)


def _gated_merge(o_attn, p_mixed, w_a, w_p, proj, gate_col0):
    n, ka = o_attn.shape
    _, kp = p_mixed.shape
    d = w_a.shape[1]
    tm, tn = _tile(1024, n), _tile(512, d)
    gb = gate_col0 // tn
    nb = d // tn
    return pl.pallas_call(
        _gated_kernel,
        grid=(n // tm, nb),
        in_specs=[pl.BlockSpec((tm, ka), lambda i, j: (i, 0)),
                  pl.BlockSpec((tm, kp), lambda i, j: (i, 0)),
                  pl.BlockSpec((ka, tn), lambda i, j: (0, j)),
                  pl.BlockSpec((kp, tn), lambda i, j: (0, j)),
                  pl.BlockSpec((tm, tn), lambda i, j: (i, gb + j)),
                  pl.BlockSpec((tm, tn), lambda i, j: (i, gb + nb + j))],
        out_specs=pl.BlockSpec((tm, tn), lambda i, j: (i, j)),
        out_shape=jax.ShapeDtypeStruct((n, d), BF16),
        compiler_params=_cparams(("parallel", "parallel")),
        name="gated_merge",
    )(o_attn, p_mixed, w_a, w_p, proj, proj)


def _split3(x):
    hi = x.astype(BF16)
    r1 = x - hi.astype(F32)
    mid = r1.astype(BF16)
    lo = (r1 - mid.astype(F32)).astype(BF16)
    return hi, mid, lo


def _norm_route_kernel(x_ref, g_ref, wr_ref, br_ref, hp_ref, ids_ref, wts_ref, *,
                       n_groups, per_group):
    x = x_ref[...]
    tm, d = x.shape
    ms = jnp.mean(x * x, axis=-1, keepdims=True)
    h = x * lax.rsqrt(ms + RMS_EPS) * g_ref[...]
    lo_bits = lax.bitcast_convert_type(h[:, :d // 2].astype(BF16).astype(F32), U32)
    hi_bits = lax.bitcast_convert_type(h[:, d // 2:].astype(BF16).astype(F32), U32)
    hp_ref[...] = (lo_bits >> 16) | hi_bits

    h3 = _split3(h)
    w3 = _split3(wr_ref[...])
    logits = br_ref[...] + jnp.zeros((tm, LANES), F32)
    for a in range(3):
        for b in range(3 - a):
            logits = logits + jnp.dot(h3[a], w3[b], preferred_element_type=F32)

    col = lax.broadcasted_iota(I32, (tm, LANES), 1).astype(F32)
    is_grp = col < n_groups
    gl = jnp.where(is_grp, logits, NEG_BIG)
    gmax = jnp.max(gl, axis=-1, keepdims=True)
    g_top = jnp.min(jnp.where(gl == gmax, col, float(LANES)), axis=-1, keepdims=True)
    denom = jnp.sum(jnp.where(is_grp, jnp.exp(gl - gmax), 0.0), axis=-1, keepdims=True)
    p_group = 1.0 / denom
    first = n_groups + g_top * per_group
    in_grp = (col >= first) & (col < first + per_group)
    el = jnp.where(in_grp, logits, NEG_BIG)
    v1 = jnp.max(el, axis=-1, keepdims=True)
    i1 = jnp.min(jnp.where(el == v1, col, float(LANES)), axis=-1, keepdims=True)
    el2 = jnp.where(col == i1, NEG_BIG, el)
    v2 = jnp.max(el2, axis=-1, keepdims=True)
    i2 = jnp.min(jnp.where(el2 == v2, col, float(LANES)), axis=-1, keepdims=True)
    e2 = jnp.exp(v2 - v1)
    w1 = p_group / (1.0 + e2)
    w2 = p_group * e2 / (1.0 + e2)
    ids = jnp.where(col == 0.0, i1 - n_groups, jnp.where(col == 1.0, i2 - n_groups, 0.0))
    ids_ref[...] = ids.astype(I32)
    wts_ref[...] = jnp.where(col == 0.0, w1, jnp.where(col == 1.0, w2, 0.0))


def _norm_route(x, g, w_rg, b_rg, w_re, b_re):
    n, d = x.shape
    n_groups = w_rg.shape[1]
    n_experts = w_re.shape[1]
    assert n_groups + n_experts <= LANES
    pad = LANES - n_groups - n_experts
    wr = jnp.concatenate([w_rg, w_re, jnp.zeros((d, pad), F32)], axis=1)
    br = jnp.concatenate([b_rg, b_re, jnp.zeros((pad,), F32)]).reshape(1, LANES)
    tm = _tile(256, n)
    kern = functools.partial(_norm_route_kernel, n_groups=n_groups,
                             per_group=n_experts // n_groups)
    return pl.pallas_call(
        kern,
        grid=(n // tm,),
        in_specs=[pl.BlockSpec((tm, d), lambda i: (i, 0)),
                  pl.BlockSpec((1, d), lambda i: (0, 0)),
                  pl.BlockSpec((d, LANES), lambda i: (0, 0)),
                  pl.BlockSpec((1, LANES), lambda i: (0, 0))],
        out_specs=[pl.BlockSpec((tm, d // 2), lambda i: (i, 0)),
                   pl.BlockSpec((tm, LANES), lambda i: (i, 0)),
                   pl.BlockSpec((tm, LANES), lambda i: (i, 0))],
        out_shape=[jax.ShapeDtypeStruct((n, d // 2), U32),
                   jax.ShapeDtypeStruct((n, LANES), I32),
                   jax.ShapeDtypeStruct((n, LANES), F32)],
        compiler_params=_cparams(("parallel",)),
        name="norm_route",
    )(x, g.reshape(1, d), wr, br)


def _gather_kernel(tok_ref, valid_ref, h_hbm, o_ref, sem, *, tm):
    t = pl.program_id(0)

    @pl.when(valid_ref[t] != 0)
    def _():
        def issue(r, c):
            tok = tok_ref[t * tm + r]
            pltpu.make_async_copy(h_hbm.at[pl.ds(tok, 1)], o_ref.at[pl.ds(r, 1)], sem).start()
            return c

        lax.fori_loop(0, tm, issue, 0)
        pltpu.make_async_copy(h_hbm.at[pl.ds(0, tm)], o_ref, sem).wait()

    @pl.when(valid_ref[t] == 0)
    def _():
        o_ref[...] = jnp.zeros_like(o_ref)


def _gather_rows(hp, row_tok, tile_valid, tm):
    rows = row_tok.shape[0]
    w = hp.shape[1]
    return pl.pallas_call(
        functools.partial(_gather_kernel, tm=tm),
        grid_spec=pltpu.PrefetchScalarGridSpec(
            num_scalar_prefetch=2,
            grid=(rows // tm,),
            in_specs=[pl.BlockSpec(memory_space=pl.ANY)],
            out_specs=pl.BlockSpec((tm, w), lambda t, tok, valid: (t, 0)),
            scratch_shapes=[pltpu.SemaphoreType.DMA(())]),
        out_shape=jax.ShapeDtypeStruct((rows, w), hp.dtype),
        compiler_params=_cparams(("arbitrary",)),
        name="moe_gather",
    )(row_tok, tile_valid, hp)


def _unpack_pairs(p):
    lo = lax.bitcast_convert_type(p << 16, F32).astype(BF16)
    hi = lax.bitcast_convert_type(p & jnp.uint32(0xFFFF0000), F32).astype(BF16)
    return lo, hi


def _expert_up_kernel(te_ref, valid_ref, x_ref, wg_ref, wu_ref, o_ref):
    t = pl.program_id(1)

    @pl.when(valid_ref[t] != 0)
    def _():
        xa, xb = _unpack_pairs(x_ref[...])
        half = xa.shape[1]
        g = (jnp.dot(xa, wg_ref[0, :half, :], preferred_element_type=F32)
             + jnp.dot(xb, wg_ref[0, half:, :], preferred_element_type=F32))
        u = (jnp.dot(xa, wu_ref[0, :half, :], preferred_element_type=F32)
             + jnp.dot(xb, wu_ref[0, half:, :], preferred_element_type=F32))
        o_ref[...] = (g * jax.nn.sigmoid(g) * u).astype(o_ref.dtype)

    @pl.when(valid_ref[t] == 0)
    def _():
        o_ref[...] = jnp.zeros_like(o_ref)


def _expert_up(xs, w_gate, w_up, tile_e, tile_valid, tm):
    rows, half = xs.shape
    _, d, f = w_gate.shape
    fc = _tile(512, f)
    nt = rows // tm
    return pl.pallas_call(
        _expert_up_kernel,
        grid_spec=pltpu.PrefetchScalarGridSpec(
            num_scalar_prefetch=2,
            grid=(f // fc, nt),
            in_specs=[pl.BlockSpec((tm, half), lambda c, t, te, tv: (t, 0)),
                      pl.BlockSpec((1, d, fc), lambda c, t, te, tv: (te[t], 0, c)),
                      pl.BlockSpec((1, d, fc), lambda c, t, te, tv: (te[t], 0, c))],
            out_specs=pl.BlockSpec((tm, fc), lambda c, t, te, tv: (t, c))),
        out_shape=jax.ShapeDtypeStruct((rows, f), BF16),
        compiler_params=_cparams(("arbitrary", "arbitrary")),
        name="expert_up",
    )(tile_e, tile_valid, xs, w_gate, w_up)


def _expert_down_kernel(te_ref, valid_ref, h_ref, wd_ref, rw_ref, o_ref):
    t = pl.program_id(1)

    @pl.when(valid_ref[t] != 0)
    def _():
        y = jnp.dot(h_ref[...], wd_ref[0], preferred_element_type=F32)
        o_ref[...] = (y * rw_ref[...]).astype(o_ref.dtype)

    @pl.when(valid_ref[t] == 0)
    def _():
        o_ref[...] = jnp.zeros_like(o_ref)


def _expert_down(hact, w_down, row_w, tile_e, tile_valid, tm):
    rows, f = hact.shape
    _, _, d = w_down.shape
    tn = _tile(2048, d)
    nt = rows // tm
    return pl.pallas_call(
        _expert_down_kernel,
        grid_spec=pltpu.PrefetchScalarGridSpec(
            num_scalar_prefetch=2,
            grid=(d // tn, nt),
            in_specs=[pl.BlockSpec((tm, f), lambda c, t, te, tv: (t, 0)),
                      pl.BlockSpec((1, f, tn), lambda c, t, te, tv: (te[t], 0, c)),
                      pl.BlockSpec((tm, 1), lambda c, t, te, tv: (t, 0))],
            out_specs=pl.BlockSpec((tm, tn), lambda c, t, te, tv: (t, c))),
        out_shape=jax.ShapeDtypeStruct((rows, d), F32),
        compiler_params=_cparams(("arbitrary", "arbitrary")),
        name="expert_down",
    )(tile_e, tile_valid, hact, w_down, row_w.reshape(rows, 1))


def _combine_kernel(pos_ref, x_ref, ys_hbm, g_ref, o_ref, buf, sem, *, tm, final_norm):
    i = pl.program_id(0)

    def issue(r, c):
        tok = i * tm + r
        for k in range(TOP_K):
            pltpu.make_async_copy(ys_hbm.at[pl.ds(pos_ref[TOP_K * tok + k], 1)],
                                  buf.at[k, pl.ds(r, 1)], sem.at[k]).start()
        return c

    lax.fori_loop(0, tm, issue, 0)
    y = x_ref[...]
    for k in range(TOP_K):
        pltpu.make_async_copy(ys_hbm.at[pl.ds(0, tm)], buf.at[k], sem.at[k]).wait()
        y = y + buf[k]
    if final_norm:
        ms = jnp.mean(y * y, axis=-1, keepdims=True)
        y = y * lax.rsqrt(ms + RMS_EPS) * g_ref[...]
    o_ref[...] = y


def _combine(x1, ys, pos, g, final_norm):
    n, d = x1.shape
    tm = _tile(128, n)
    return pl.pallas_call(
        functools.partial(_combine_kernel, tm=tm, final_norm=final_norm),
        grid_spec=pltpu.PrefetchScalarGridSpec(
            num_scalar_prefetch=1,
            grid=(n // tm,),
            in_specs=[pl.BlockSpec((tm, d), lambda i, pos: (i, 0)),
                      pl.BlockSpec(memory_space=pl.ANY),
                      pl.BlockSpec((1, d), lambda i, pos: (0, 0))],
            out_specs=pl.BlockSpec((tm, d), lambda i, pos: (i, 0)),
            scratch_shapes=[pltpu.VMEM((TOP_K, tm, d), F32),
                            pltpu.SemaphoreType.DMA((TOP_K,))]),
        out_shape=jax.ShapeDtypeStruct((n, d), F32),
        compiler_params=_cparams(("arbitrary",)),
        name="moe_combine",
    )(pos, x1, ys, g.reshape(1, d))


def _dispatch_plan(expert_ids, weights, n_experts, tm):
    n = expert_ids.shape[0]
    a = n * TOP_K
    flat_e = expert_ids.reshape(a)
    onehot = (flat_e[:, None] == jnp.arange(n_experts, dtype=I32)[None, :]).astype(I32)
    incl = jnp.cumsum(onehot, axis=0)
    rank = jnp.sum((incl - onehot) * onehot, axis=1)
    counts = incl[-1]
    padded = (counts + tm - 1) // tm * tm
    pends = jnp.cumsum(padded)
    pstarts = pends - padded
    dest = (pstarts[flat_e] + rank).astype(I32)
    n_tiles = a // tm + n_experts
    rows = n_tiles * tm
    row_tok = jnp.zeros((rows,), I32).at[dest].set(jnp.arange(a, dtype=I32) // TOP_K)
    row_w = jnp.zeros((rows,), F32).at[dest].set(weights.reshape(a))
    tile_start = jnp.arange(n_tiles, dtype=I32) * tm
    tile_e = jnp.clip(jnp.searchsorted(pends, tile_start, side='right'), 0, n_experts - 1).astype(I32)
    tile_valid = (tile_start < pends[-1]).astype(I32)
    return row_tok, row_w, dest, tile_e, tile_valid


def kernel(x, norm_mix, w_in, w_o_attn, pool_w, pool_scale, w_o_pool, w_out, norm_ffn,
           w_router_group, b_router_group, w_router_expert, b_router_expert,
           w_gate, w_up, w_down, norm_final):
    batch, seq, d = x.shape
    n = batch * seq
    depth = norm_mix.shape[0]
    sb_width = w_o_attn.shape[1]
    n_heads = sb_width // HEAD_DIM
    groups, gd = pool_w.shape[1], pool_w.shape[2]
    pool_width = groups * gd
    n_experts = w_router_expert.shape[2]
    moe_tm = 256
    assert pool_width == sb_width and (3 * sb_width) % pool_width == 0

    xf = x.reshape(n, d)
    for l in range(depth):
        h = _rmsnorm(xf, norm_mix[l], BF16)
        proj = _matmul(h, w_in[l].astype(BF16), BF16, tm=1024, tn=1024, name="in_proj")
        o_attn = _attention(proj, batch, seq, n_heads)
        p_mixed = _pool_mix(proj, pool_w[l].astype(BF16), pool_scale[l], batch, seq,
                            (3 * sb_width) // pool_width)
        merged = _gated_merge(o_attn, p_mixed, w_o_attn[l].astype(BF16), w_o_pool[l].astype(BF16),
                              proj, 3 * sb_width + pool_width)
        x1 = _matmul(merged, w_out[l].astype(BF16), F32, tm=512, tn=1024, residual=xf,
                     name="out_proj")
        hp, ids_slab, wts_slab = _norm_route(x1, norm_ffn[l], w_router_group[l], b_router_group[l],
                                             w_router_expert[l], b_router_expert[l])
        row_tok, row_w, pos, tile_e, tile_valid = _dispatch_plan(
            ids_slab[:, :TOP_K], wts_slab[:, :TOP_K], n_experts, moe_tm)
        xs = _gather_rows(hp, row_tok, tile_valid, moe_tm)
        hact = _expert_up(xs, w_gate[l].astype(BF16), w_up[l].astype(BF16), tile_e, tile_valid, moe_tm)
        ys = _expert_down(hact, w_down[l].astype(BF16), row_w, tile_e, tile_valid, moe_tm)
        last = l == depth - 1
        xf = _combine(x1, ys, pos, norm_final, final_norm=last)
    return xf.reshape(batch, seq, d)
```

```python
import functools
import math

import jax
import jax.numpy as jnp
from jax import lax
from jax.experimental import pallas as pl
from jax.experimental.pallas import tpu as pltpu

F32 = jnp.float32
BF16 = jnp.bfloat16
U32 = jnp.uint32
I32 = jnp.int32

HEAD_DIM = 128
POOL_SIZES = (2, 4, 8, 16)
POOL_HALO = 16
TOP_K = 2
RMS_EPS = 1e-6
LANES = 128
VMEM_LIMIT = 56 * 1024 * 1024
NEG_BIG = -1e30


def _cparams(sem):
    return pltpu.CompilerParams(dimension_semantics=sem, vmem_limit_bytes=VMEM_LIMIT)


def _tile(pref, dim):
    t = min(pref, dim)
    assert dim % t == 0, (pref, dim)
    return t


def _rmsnorm_kernel(x_ref, g_ref, o_ref):
    x = x_ref[...]
    ms = jnp.mean(x * x, axis=-1, keepdims=True)
    o_ref[...] = (x * lax.rsqrt(ms + RMS_EPS) * g_ref[...]).astype(o_ref.dtype)


def _rmsnorm(x, g, out_dtype):
    n, d = x.shape
    tm = _tile(256, n)
    return pl.pallas_call(
        _rmsnorm_kernel,
        grid=(n // tm,),
        in_specs=[pl.BlockSpec((tm, d), lambda i: (i, 0)),
                  pl.BlockSpec((1, d), lambda i: (0, 0))],
        out_specs=pl.BlockSpec((tm, d), lambda i: (i, 0)),
        out_shape=jax.ShapeDtypeStruct((n, d), out_dtype),
        compiler_params=_cparams(("parallel",)),
        name="rmsnorm",
    )(x, g.reshape(1, d))


def _mm_kernel(a_ref, b_ref, o_ref):
    o_ref[...] = jnp.dot(a_ref[...], b_ref[...], preferred_element_type=F32).astype(o_ref.dtype)


def _mm_res_kernel(a_ref, b_ref, r_ref, o_ref):
    o_ref[...] = (r_ref[...] + jnp.dot(a_ref[...], b_ref[...], preferred_element_type=F32)
                  ).astype(o_ref.dtype)


def _matmul(a, b, out_dtype, *, tm, tn, residual=None, name):
    m, k = a.shape
    _, n = b.shape
    tm, tn = _tile(tm, m), _tile(tn, n)
    in_specs = [pl.BlockSpec((tm, k), lambda i, j: (i, 0)),
                pl.BlockSpec((k, tn), lambda i, j: (0, j))]
    args = [a, b]
    body = _mm_kernel
    if residual is not None:
        in_specs.append(pl.BlockSpec((tm, tn), lambda i, j: (i, j)))
        args.append(residual)
        body = _mm_res_kernel
    return pl.pallas_call(
        body,
        grid=(m // tm, n // tn),
        in_specs=in_specs,
        out_specs=pl.BlockSpec((tm, tn), lambda i, j: (i, j)),
        out_shape=jax.ShapeDtypeStruct((m, n), out_dtype),
        compiler_params=_cparams(("parallel", "parallel")),
        name=name,
    )(*args)


ATTN_SUB = 256


def _attn_kernel(q_ref, k_ref, v_ref, o_ref, acc_ref, carry_ref, *, tq, sub, scale):
    i = pl.program_id(2)
    nsub = tq // sub
    qi = lax.broadcasted_iota(I32, (sub, sub), 0)
    ki = lax.broadcasted_iota(I32, (sub, sub), 1)
    tri = jnp.where(qi >= ki, 1.0, 0.0).astype(BF16)
    causal = ki < qi

    def chunk(q, start, masked, carry, acc):
        start = pl.multiple_of(start, sub)
        kc = k_ref[pl.ds(start, sub), :]
        vc = v_ref[pl.ds(start, sub), :]
        z = lax.dot_general(q, kc, (((1,), (1,)), ((), ())), preferred_element_type=F32) * scale
        sp = jnp.maximum(z, 0.0) + jnp.log(1.0 + jnp.exp(-jnp.abs(z)))
        if masked:
            sp = jnp.where(causal, sp, 0.0)
        cs = jnp.dot(sp.astype(BF16), tri, preferred_element_type=F32)
        w = jnp.concatenate(
            [jnp.exp(z[:, c:c + LANES] - cs[:, c:c + LANES] - carry) for c in range(0, sub, LANES)],
            axis=1)
        if masked:
            w = jnp.where(causal, w, 0.0)
        acc = acc + jnp.dot(w.astype(BF16), vc, preferred_element_type=F32)
        carry = carry + jnp.broadcast_to(jnp.sum(sp, axis=-1, keepdims=True), carry.shape)
        return carry, acc

    base = i * tq
    for rb in range(nsub):
        rows = slice(rb * sub, (rb + 1) * sub)
        q = q_ref[rows, :]
        zero = jnp.zeros((sub, HEAD_DIM), F32)
        carry, acc = chunk(q, base + rb * sub, True, zero, zero)
        for cb in range(rb - 1, -1, -1):
            carry, acc = chunk(q, base + cb * sub, False, carry, acc)
        acc_ref[rows, :] = acc
        carry_ref[rows, :] = carry

    def body(n, c):
        start0 = (i - 1 - n) * tq
        carry, acc = carry_ref[...], acc_ref[...]
        q = q_ref[...]
        for cb in range(nsub - 1, -1, -1):
            carry, acc = chunk(q, start0 + cb * sub, False, carry, acc)
        carry_ref[...] = carry
        acc_ref[...] = acc
        return c

    lax.fori_loop(0, i, body, 0)
    o_ref[...] = acc_ref[...].astype(o_ref.dtype)


def _attention(proj, batch, seq, n_heads):
    n = batch * seq
    tq = _tile(512, seq)
    sub = _tile(ATTN_SUB, tq)
    nq = seq // tq
    kern = functools.partial(_attn_kernel, tq=tq, sub=sub, scale=1.0 / math.sqrt(HEAD_DIM))
    return pl.pallas_call(
        kern,
        grid=(batch, n_heads, nq),
        in_specs=[pl.BlockSpec((tq, HEAD_DIM), lambda b, h, i: (b * nq + i, h)),
                  pl.BlockSpec((seq, HEAD_DIM), lambda b, h, i: (b, n_heads + h)),
                  pl.BlockSpec((seq, HEAD_DIM), lambda b, h, i: (b, 2 * n_heads + h))],
        out_specs=pl.BlockSpec((tq, HEAD_DIM), lambda b, h, i: (b * nq + i, h)),
        out_shape=jax.ShapeDtypeStruct((n, n_heads * HEAD_DIM), BF16),
        scratch_shapes=[pltpu.VMEM((tq, HEAD_DIM), F32), pltpu.VMEM((tq, HEAD_DIM), F32)],
        compiler_params=_cparams(("parallel", "parallel", "arbitrary")),
        name="stickbreak_attn",
    )(proj, proj, proj)


def _pool_kernel(u_ref, halo_ref, w_ref, s_ref, o_ref, buf, *, ts, gd):
    i = pl.program_id(1)
    pos = i * ts + lax.broadcasted_iota(I32, (ts, gd), 0)
    for g, win in enumerate(POOL_SIZES):
        cols = slice(g * gd, (g + 1) * gd)
        u = u_ref[:, cols].astype(F32)
        halo = halo_ref[:, cols].astype(F32)
        buf[0:POOL_HALO, :] = jnp.where(i > 0, halo, 0.0)
        buf[POOL_HALO:POOL_HALO + ts, :] = u
        acc = u
        for k in range(1, win):
            acc = acc + buf[POOL_HALO - k:POOL_HALO - k + ts, :]
        cnt = jnp.minimum(pos + 1, win).astype(F32)
        pooled = acc / cnt - u
        mixed = jnp.dot(pooled.astype(BF16), w_ref[g], preferred_element_type=F32) * s_ref[g]
        o_ref[:, cols] = mixed.astype(o_ref.dtype)


def _pool_mix(proj, pool_w, pool_scale, batch, seq, col_block):
    n = batch * seq
    groups, gd, _ = pool_w.shape
    width = groups * gd
    ts = _tile(512, seq)
    ns = seq // ts
    hb = ts // POOL_HALO
    kern = functools.partial(_pool_kernel, ts=ts, gd=gd)
    return pl.pallas_call(
        kern,
        grid=(batch, ns),
        in_specs=[pl.BlockSpec((ts, width), lambda b, i: (b * ns + i, col_block)),
                  pl.BlockSpec((POOL_HALO, width),
                               lambda b, i: (jnp.maximum((b * ns + i) * hb - 1, 0), col_block)),
                  pl.BlockSpec((groups, gd, gd), lambda b, i: (0, 0, 0)),
                  pl.BlockSpec((groups, 1, gd), lambda b, i: (0, 0, 0))],
        out_specs=pl.BlockSpec((ts, width), lambda b, i: (b * ns + i, 0)),
        out_shape=jax.ShapeDtypeStruct((n, width), BF16),
        scratch_shapes=[pltpu.VMEM((POOL_HALO + ts, gd), F32)],
        compiler_params=_cparams(("parallel", "arbitrary")),
        name="pool_mix",
    )(proj, proj, pool_w, pool_scale.reshape(groups, 1, gd))


def _gated_kernel(oa_ref, pm_ref, wa_ref, wp_ref, ga_ref, gp_ref, o_ref):
    ya = jnp.dot(oa_ref[...], wa_ref[...], preferred_element_type=F32)
    yp = jnp.dot(pm_ref[...], wp_ref[...], preferred_element_type=F32)
    ga = jax.nn.sigmoid(ga_ref[...].astype(F32))
    gp = jax.nn.sigmoid(gp_ref[...].astype(F32))
    o_ref[...] = (ga * ya + gp * yp).astype(o_ref.dtype)


def _gated_merge(o_attn, p_mixed, w_a, w_p, proj, gate_col0):
    n, ka = o_attn.shape
    _, kp = p_mixed.shape
    d = w_a.shape[1]
    tm, tn = _tile(1024, n), _tile(512, d)
    gb = gate_col0 // tn
    nb = d // tn
    return pl.pallas_call(
        _gated_kernel,
        grid=(n // tm, nb),
        in_specs=[pl.BlockSpec((tm, ka), lambda i, j: (i, 0)),
                  pl.BlockSpec((tm, kp), lambda i, j: (i, 0)),
                  pl.BlockSpec((ka, tn), lambda i, j: (0, j)),
                  pl.BlockSpec((kp, tn), lambda i, j: (0, j)),
                  pl.BlockSpec((tm, tn), lambda i, j: (i, gb + j)),
                  pl.BlockSpec((tm, tn), lambda i, j: (i, gb + nb + j))],
        out_specs=pl.BlockSpec((tm, tn), lambda i, j: (i, j)),
        out_shape=jax.ShapeDtypeStruct((n, d), BF16),
        compiler_params=_cparams(("parallel", "parallel")),
        name="gated_merge",
    )(o_attn, p_mixed, w_a, w_p, proj, proj)


def _split3(x):
    hi = x.astype(BF16)
    r1 = x - hi.astype(F32)
    mid = r1.astype(BF16)
    lo = (r1 - mid.astype(F32)).astype(BF16)
    return hi, mid, lo


def _norm_route_kernel(x_ref, g_ref, wr_ref, br_ref, hp_ref, ids_ref, wts_ref, *,
                       n_groups, per_group):
    x = x_ref[...]
    tm, d = x.shape
    ms = jnp.mean(x * x, axis=-1, keepdims=True)
    h = x * lax.rsqrt(ms + RMS_EPS) * g_ref[...]
    hp_ref[...] = h

    h3 = _split3(h)
    w3 = _split3(wr_ref[...])
    logits = br_ref[...] + jnp.zeros((tm, LANES), F32)
    for a in range(3):
        for b in range(3 - a):
            logits = logits + jnp.dot(h3[a], w3[b], preferred_element_type=F32)

    col = lax.broadcasted_iota(I32, (tm, LANES), 1).astype(F32)
    is_grp = col < n_groups
    gl = jnp.where(is_grp, logits, NEG_BIG)
    gmax = jnp.max(gl, axis=-1, keepdims=True)
    g_top = jnp.min(jnp.where(gl == gmax, col, float(LANES)), axis=-1, keepdims=True)
    denom = jnp.sum(jnp.where(is_grp, jnp.exp(gl - gmax), 0.0), axis=-1, keepdims=True)
    p_group = 1.0 / denom
    first = n_groups + g_top * per_group
    in_grp = (col >= first) & (col < first + per_group)
    el = jnp.where(in_grp, logits, NEG_BIG)
    v1 = jnp.max(el, axis=-1, keepdims=True)
    i1 = jnp.min(jnp.where(el == v1, col, float(LANES)), axis=-1, keepdims=True)
    el2 = jnp.where(col == i1, NEG_BIG, el)
    v2 = jnp.max(el2, axis=-1, keepdims=True)
    i2 = jnp.min(jnp.where(el2 == v2, col, float(LANES)), axis=-1, keepdims=True)
    e2 = jnp.exp(v2 - v1)
    w1 = p_group / (1.0 + e2)
    w2 = p_group * e2 / (1.0 + e2)
    ids = jnp.where(col == 0.0, i1 - n_groups, jnp.where(col == 1.0, i2 - n_groups, 0.0))
    ids_ref[...] = ids.astype(I32)
    wts_ref[...] = jnp.where(col == 0.0, w1, jnp.where(col == 1.0, w2, 0.0))


def _norm_route(x, g, w_rg, b_rg, w_re, b_re):
    n, d = x.shape
    n_groups = w_rg.shape[1]
    n_experts = w_re.shape[1]
    assert n_groups + n_experts <= LANES
    pad = LANES - n_groups - n_experts
    wr = jnp.concatenate([w_rg, w_re, jnp.zeros((d, pad), F32)], axis=1)
    br = jnp.concatenate([b_rg, b_re, jnp.zeros((pad,), F32)]).reshape(1, LANES)
    tm = _tile(256, n)
    kern = functools.partial(_norm_route_kernel, n_groups=n_groups,
                             per_group=n_experts // n_groups)
    return pl.pallas_call(
        kern,
        grid=(n // tm,),
        in_specs=[pl.BlockSpec((tm, d), lambda i: (i, 0)),
                  pl.BlockSpec((1, d), lambda i: (0, 0)),
                  pl.BlockSpec((d, LANES), lambda i: (0, 0)),
                  pl.BlockSpec((1, LANES), lambda i: (0, 0))],
        out_specs=[pl.BlockSpec((tm, d), lambda i: (i, 0)),
                   pl.BlockSpec((tm, LANES), lambda i: (i, 0)),
                   pl.BlockSpec((tm, LANES), lambda i: (i, 0))],
        out_shape=[jax.ShapeDtypeStruct((n, d), F32),
                   jax.ShapeDtypeStruct((n, LANES), I32),
                   jax.ShapeDtypeStruct((n, LANES), F32)],
        compiler_params=_cparams(("parallel",)),
        name="norm_route",
    )(x, g.reshape(1, d), wr, br)


def _start_row_copies(idx_ref, base, src_hbm, dst, sem, count):
    for r in range(count):
        pltpu.make_async_copy(src_hbm.at[pl.ds(idx_ref[base + r], 1)], dst.at[pl.ds(r, 1)], sem).start()


def _expert_up_kernel(tok_ref, te_ref, valid_ref, h_hbm, wg_ref, wu_ref, o_ref, xbuf, sem, *, tm):
    t = pl.program_id(0)
    last = pl.num_programs(0) - 1
    slot = t % 2
    valid = valid_ref[t] != 0

    @pl.when(t == 0)
    def _():
        _start_row_copies(tok_ref, 0, h_hbm, xbuf.at[0], sem.at[0], tm)

    pltpu.make_async_copy(h_hbm.at[pl.ds(0, tm)], xbuf.at[slot], sem.at[slot]).wait()

    def prefetch():
        _start_row_copies(tok_ref, (t + 1) * tm, h_hbm, xbuf.at[1 - slot], sem.at[1 - slot], tm)

    def ffn():
        x = xbuf[slot].astype(BF16)
        g = jnp.dot(x, wg_ref[0], preferred_element_type=F32)
        u = jnp.dot(x, wu_ref[0], preferred_element_type=F32)
        o_ref[...] = (g * jax.nn.sigmoid(g) * u).astype(o_ref.dtype)

    def skip():
        o_ref[...] = jnp.zeros_like(o_ref)

    for more, live in ((True, True), (True, False), (False, True), (False, False)):
        @pl.when(((t < last) == more) & (valid == live))
        def _(more=more, live=live):
            if more:
                prefetch()
            ffn() if live else skip()


def _expert_up(h, row_tok, w_gate, w_up, tile_e, tile_valid, tm):
    rows = row_tok.shape[0]
    _, d, f = w_gate.shape
    nt = rows // tm
    return pl.pallas_call(
        functools.partial(_expert_up_kernel, tm=tm),
        grid_spec=pltpu.PrefetchScalarGridSpec(
            num_scalar_prefetch=3,
            grid=(nt,),
            in_specs=[pl.BlockSpec(memory_space=pl.ANY),
                      pl.BlockSpec((1, d, f), lambda t, tok, te, tv: (te[t], 0, 0)),
                      pl.BlockSpec((1, d, f), lambda t, tok, te, tv: (te[t], 0, 0))],
            out_specs=pl.BlockSpec((tm, f), lambda t, tok, te, tv: (t, 0)),
            scratch_shapes=[pltpu.VMEM((2, tm, d), F32), pltpu.SemaphoreType.DMA((2,))]),
        out_shape=jax.ShapeDtypeStruct((rows, f), BF16),
        compiler_params=_cparams(("arbitrary",)),
        name="expert_up",
    )(row_tok, tile_e, tile_valid, h, w_gate, w_up)


def _expert_down_kernel(te_ref, valid_ref, h_ref, wd_ref, rw_ref, o_ref):
    t = pl.program_id(0)

    @pl.when(valid_ref[t] != 0)
    def _():
        y = jnp.dot(h_ref[...], wd_ref[0], preferred_element_type=F32)
        o_ref[...] = (y * rw_ref[...]).astype(o_ref.dtype)

    @pl.when(valid_ref[t] == 0)
    def _():
        o_ref[...] = jnp.zeros_like(o_ref)


def _expert_down(hact, w_down, row_w, tile_e, tile_valid, tm):
    rows, f = hact.shape
    _, _, d = w_down.shape
    nt = rows // tm
    return pl.pallas_call(
        _expert_down_kernel,
        grid_spec=pltpu.PrefetchScalarGridSpec(
            num_scalar_prefetch=2,
            grid=(nt,),
            in_specs=[pl.BlockSpec((tm, f), lambda t, te, tv: (t, 0)),
                      pl.BlockSpec((1, f, d), lambda t, te, tv: (te[t], 0, 0)),
                      pl.BlockSpec((tm, 1), lambda t, te, tv: (t, 0))],
            out_specs=pl.BlockSpec((tm, d), lambda t, te, tv: (t, 0))),
        out_shape=jax.ShapeDtypeStruct((rows, d), F32),
        compiler_params=_cparams(("arbitrary",)),
        name="expert_down",
    )(tile_e, tile_valid, hact, w_down, row_w.reshape(rows, 1))


COMBINE_ROWS = 8


def _combine_kernel(pos_ref, x_ref, ys_hbm, g_ref, o_ref, buf, sem, *, tm, final_norm):
    i = pl.program_id(0)
    last = pl.num_programs(0) - 1
    slot = i % 2

    def gather(tile, s):
        for k in range(TOP_K):
            for r in range(tm):
                p = pos_ref[TOP_K * (tile * tm + r) + k]
                pltpu.make_async_copy(ys_hbm.at[pl.ds(p, 1)], buf.at[s, k, pl.ds(r, 1)],
                                      sem.at[s, k]).start()

    @pl.when(i == 0)
    def _():
        gather(0, 0)

    for k in range(TOP_K):
        pltpu.make_async_copy(ys_hbm.at[pl.ds(0, tm)], buf.at[slot, k], sem.at[slot, k]).wait()

    def finish():
        for c in range(0, tm, COMBINE_ROWS):
            rows = slice(c, c + COMBINE_ROWS)
            y = x_ref[rows, :]
            for k in range(TOP_K):
                y = y + buf[slot, k, rows, :]
            if final_norm:
                ms = jnp.mean(y * y, axis=-1, keepdims=True)
                y = y * lax.rsqrt(ms + RMS_EPS) * g_ref[...]
            o_ref[rows, :] = y

    @pl.when(i < last)
    def _():
        gather(i + 1, 1 - slot)
        finish()

    @pl.when(i == last)
    def _():
        finish()


def _combine(x1, ys, pos, g, final_norm):
    n, d = x1.shape
    tm = _tile(128, n)
    return pl.pallas_call(
        functools.partial(_combine_kernel, tm=tm, final_norm=final_norm),
        grid_spec=pltpu.PrefetchScalarGridSpec(
            num_scalar_prefetch=1,
            grid=(n // tm,),
            in_specs=[pl.BlockSpec((tm, d), lambda i, pos: (i, 0)),
                      pl.BlockSpec(memory_space=pl.ANY),
                      pl.BlockSpec((1, d), lambda i, pos: (0, 0))],
            out_specs=pl.BlockSpec((tm, d), lambda i, pos: (i, 0)),
            scratch_shapes=[pltpu.VMEM((2, TOP_K, tm, d), F32),
                            pltpu.SemaphoreType.DMA((2, TOP_K))]),
        out_shape=jax.ShapeDtypeStruct((n, d), F32),
        compiler_params=_cparams(("arbitrary",)),
        name="moe_combine",
    )(pos, x1, ys, g.reshape(1, d))


def _dispatch_plan(expert_ids, weights, n_experts, tm):
    n = expert_ids.shape[0]
    a = n * TOP_K
    flat_e = expert_ids.reshape(a)
    onehot = (flat_e[:, None] == jnp.arange(n_experts, dtype=I32)[None, :]).astype(I32)
    incl = jnp.cumsum(onehot, axis=0)
    rank = jnp.sum((incl - onehot) * onehot, axis=1)
    counts = incl[-1]
    padded = (counts + tm - 1) // tm * tm
    pends = jnp.cumsum(padded)
    pstarts = pends - padded
    dest = (pstarts[flat_e] + rank).astype(I32)
    n_tiles = a // tm + n_experts
    rows = n_tiles * tm
    fields = jnp.stack([(jnp.arange(a, dtype=I32) // TOP_K).astype(F32), weights.reshape(a)], axis=1)
    row_fields = jnp.zeros((rows, 2), F32).at[dest].set(fields)
    row_tok = row_fields[:, 0].astype(I32)
    row_w = row_fields[:, 1]
    tile_start = jnp.arange(n_tiles, dtype=I32) * tm
    tile_e = jnp.minimum(jnp.sum((tile_start[:, None] >= pends[None, :]).astype(I32), axis=1),
                         n_experts - 1)
    tile_valid = (tile_start < pends[-1]).astype(I32)
    return row_tok, row_w, dest, tile_e, tile_valid


def kernel(x, norm_mix, w_in, w_o_attn, pool_w, pool_scale, w_o_pool, w_out, norm_ffn,
           w_router_group, b_router_group, w_router_expert, b_router_expert,
           w_gate, w_up, w_down, norm_final):
    batch, seq, d = x.shape
    n = batch * seq
    depth = norm_mix.shape[0]
    sb_width = w_o_attn.shape[1]
    n_heads = sb_width // HEAD_DIM
    groups, gd = pool_w.shape[1], pool_w.shape[2]
    pool_width = groups * gd
    n_experts = w_router_expert.shape[2]
    moe_tm = 256
    assert pool_width == sb_width and (3 * sb_width) % pool_width == 0

    xf = x.reshape(n, d)
    for l in range(depth):
        h = _rmsnorm(xf, norm_mix[l], BF16)
        proj = _matmul(h, w_in[l].astype(BF16), BF16, tm=1024, tn=1024, name="in_proj")
        o_attn = _attention(proj, batch, seq, n_heads)
        p_mixed = _pool_mix(proj, pool_w[l].astype(BF16), pool_scale[l], batch, seq,
                            (3 * sb_width) // pool_width)
        merged = _gated_merge(o_attn, p_mixed, w_o_attn[l].astype(BF16), w_o_pool[l].astype(BF16),
                              proj, 3 * sb_width + pool_width)
        x1 = _matmul(merged, w_out[l].astype(BF16), F32, tm=512, tn=1024, residual=xf,
                     name="out_proj")
        hp, ids_slab, wts_slab = _norm_route(x1, norm_ffn[l], w_router_group[l], b_router_group[l],
                                             w_router_expert[l], b_router_expert[l])
        row_tok, row_w, pos, tile_e, tile_valid = _dispatch_plan(
            ids_slab[:, :TOP_K], wts_slab[:, :TOP_K], n_experts, moe_tm)
        hact = _expert_up(hp, row_tok, w_gate[l].astype(BF16), w_up[l].astype(BF16), tile_e, tile_valid,
                          moe_tm)
        ys = _expert_down(hact, w_down[l].astype(BF16), row_w, tile_e, tile_valid, moe_tm)
        last = l == depth - 1
        xf = _combine(x1, ys, pos, norm_final, final_norm=last)
    return xf.reshape(batch, seq, d)
```

```python
import functools
import math

import jax
import jax.numpy as jnp
from jax import lax
from jax.experimental import pallas as pl
from jax.experimental.pallas import tpu as pltpu

F32 = jnp.float32
BF16 = jnp.bfloat16
U32 = jnp.uint32
I32 = jnp.int32

HEAD_DIM = 128
POOL_SIZES = (2, 4, 8, 16)
POOL_HALO = 16
TOP_K = 2
RMS_EPS = 1e-6
LANES = 128
VMEM_LIMIT = 56 * 1024 * 1024
NEG_BIG = -1e30


def _cparams(sem):
    return pltpu.CompilerParams(dimension_semantics=sem, vmem_limit_bytes=VMEM_LIMIT)


def _tile(pref, dim):
    t = min(pref, dim)
    assert dim % t == 0, (pref, dim)
    return t


def _rmsnorm_kernel(x_ref, g_ref, o_ref):
    x = x_ref[...]
    ms = jnp.mean(x * x, axis=-1, keepdims=True)
    o_ref[...] = (x * lax.rsqrt(ms + RMS_EPS) * g_ref[...]).astype(o_ref.dtype)


def _rmsnorm(x, g, out_dtype):
    n, d = x.shape
    tm = _tile(256, n)
    return pl.pallas_call(
        _rmsnorm_kernel,
        grid=(n // tm,),
        in_specs=[pl.BlockSpec((tm, d), lambda i: (i, 0)),
                  pl.BlockSpec((1, d), lambda i: (0, 0))],
        out_specs=pl.BlockSpec((tm, d), lambda i: (i, 0)),
        out_shape=jax.ShapeDtypeStruct((n, d), out_dtype),
        compiler_params=_cparams(("parallel",)),
        name="rmsnorm",
    )(x, g.reshape(1, d))


def _mm_kernel(a_ref, b_ref, o_ref):
    o_ref[...] = jnp.dot(a_ref[...], b_ref[...], preferred_element_type=F32).astype(o_ref.dtype)


def _mm_res_kernel(a_ref, b_ref, r_ref, o_ref):
    o_ref[...] = (r_ref[...] + jnp.dot(a_ref[...], b_ref[...], preferred_element_type=F32)
                  ).astype(o_ref.dtype)


def _matmul(a, b, out_dtype, *, tm, tn, residual=None, name):
    m, k = a.shape
    _, n = b.shape
    tm, tn = _tile(tm, m), _tile(tn, n)
    in_specs = [pl.BlockSpec((tm, k), lambda i, j: (i, 0)),
                pl.BlockSpec((k, tn), lambda i, j: (0, j))]
    args = [a, b]
    body = _mm_kernel
    if residual is not None:
        in_specs.append(pl.BlockSpec((tm, tn), lambda i, j: (i, j)))
        args.append(residual)
        body = _mm_res_kernel
    return pl.pallas_call(
        body,
        grid=(m // tm, n // tn),
        in_specs=in_specs,
        out_specs=pl.BlockSpec((tm, tn), lambda i, j: (i, j)),
        out_shape=jax.ShapeDtypeStruct((m, n), out_dtype),
        compiler_params=_cparams(("parallel", "parallel")),
        name=name,
    )(*args)


ATTN_SUB = 256
ATTN_HEADS = 2


def _attn_kernel(q_ref, k_ref, v_ref, o_ref, acc_ref, carry_ref, *, tq, sub, heads):
    i = pl.program_id(2)
    nsub = tq // sub
    qi = lax.broadcasted_iota(I32, (sub, sub), 0)
    ki = lax.broadcasted_iota(I32, (sub, sub), 1)
    tri = jnp.where(qi >= ki, 1.0, 0.0).astype(BF16)
    causal = ki < qi

    def attend(rows, start, nkeys, masked, carries, accs):
        nrows = rows.stop - rows.start
        start = pl.multiple_of(start, sub)
        hcols = [slice(h * HEAD_DIM, (h + 1) * HEAD_DIM) for h in range(heads)]
        z = [lax.dot_general(q_ref[rows, hc], k_ref[pl.ds(start, nkeys), hc],
                             (((1,), (1,)), ((), ())), preferred_element_type=F32) for hc in hcols]
        sp = [jnp.maximum(zh, 0.0) + jnp.log2(1.0 + jnp.exp2(-jnp.abs(zh))) for zh in z]
        if masked:
            sp = [jnp.where(causal, s, 0.0) for s in sp]
        chunks = [slice(c, c + sub) for c in range(0, nkeys, sub)]
        spb = [s.astype(BF16) for s in sp]
        cs = [[jnp.dot(s[:, ck], tri, preferred_element_type=F32) for ck in chunks] for s in spb]
        tot = [[jnp.broadcast_to(jnp.sum(s[:, ck], axis=-1, keepdims=True), (nrows, HEAD_DIM))
                for ck in chunks] for s in sp]
        new_carries, new_accs = [], []
        for h in range(heads):
            carry = carries[h]
            w = [None] * (nkeys // LANES)
            for cb in range(len(chunks) - 1, -1, -1):
                for c0 in range(0, sub, LANES):
                    g0 = cb * sub + c0
                    w[g0 // LANES] = jnp.exp2(z[h][:, g0:g0 + LANES] - cs[h][cb][:, c0:c0 + LANES]
                                              - carry)
                carry = carry + tot[h][cb]
            w = jnp.concatenate(w, axis=1)
            if masked:
                w = jnp.where(causal, w, 0.0)
            new_carries.append(carry)
            new_accs.append(accs[h] + jnp.dot(w.astype(BF16), v_ref[pl.ds(start, nkeys), hcols[h]],
                                              preferred_element_type=F32))
        return new_carries, new_accs

    base = i * tq
    for rb in range(nsub):
        rows = slice(rb * sub, (rb + 1) * sub)
        zero = [jnp.zeros((sub, HEAD_DIM), F32)] * heads
        carries, accs = attend(rows, base + rb * sub, sub, True, zero, zero)
        for cb in range(rb - 1, -1, -1):
            carries, accs = attend(rows, base + cb * sub, sub, False, carries, accs)
        acc_ref[rows, :] = jnp.concatenate(accs, axis=1)
        carry_ref[rows, :] = jnp.concatenate(carries, axis=1)

    def body(n, c):
        full = slice(0, tq)
        hcols = [slice(h * HEAD_DIM, (h + 1) * HEAD_DIM) for h in range(heads)]
        carries, accs = attend(full, (i - 1 - n) * tq, tq, False,
                               [carry_ref[:, hc] for hc in hcols], [acc_ref[:, hc] for hc in hcols])
        acc_ref[...] = jnp.concatenate(accs, axis=1)
        carry_ref[...] = jnp.concatenate(carries, axis=1)
        return c

    lax.fori_loop(0, i, body, 0)
    o_ref[...] = acc_ref[...].astype(o_ref.dtype)


def _attention(proj, batch, seq, n_heads):
    n = batch * seq
    tq = _tile(512, seq)
    sub = _tile(ATTN_SUB, tq)
    nq = seq // tq
    heads = _tile(ATTN_HEADS, n_heads)
    ng = n_heads // heads
    width = heads * HEAD_DIM
    kern = functools.partial(_attn_kernel, tq=tq, sub=sub, heads=heads)
    return pl.pallas_call(
        kern,
        grid=(batch, ng, nq),
        in_specs=[pl.BlockSpec((tq, width), lambda b, g, i: (b * nq + i, g)),
                  pl.BlockSpec((seq, width), lambda b, g, i: (b, ng + g)),
                  pl.BlockSpec((seq, width), lambda b, g, i: (b, 2 * ng + g))],
        out_specs=pl.BlockSpec((tq, width), lambda b, g, i: (b * nq + i, g)),
        out_shape=jax.ShapeDtypeStruct((n, n_heads * HEAD_DIM), BF16),
        scratch_shapes=[pltpu.VMEM((tq, width), F32), pltpu.VMEM((tq, width), F32)],
        compiler_params=_cparams(("parallel", "parallel", "arbitrary")),
        name="stickbreak_attn",
    )(proj, proj, proj)


def _pool_kernel(u_ref, halo_ref, w_ref, s_ref, o_ref, buf, *, ts, gd):
    i = pl.program_id(1)
    pos = i * ts + lax.broadcasted_iota(I32, (ts, gd), 0)
    for g, win in enumerate(POOL_SIZES):
        cols = slice(g * gd, (g + 1) * gd)
        u = u_ref[:, cols].astype(F32)
        halo = halo_ref[:, cols].astype(F32)
        buf[0:POOL_HALO, :] = jnp.where(i > 0, halo, 0.0)
        buf[POOL_HALO:POOL_HALO + ts, :] = u
        acc = u
        for k in range(1, win):
            acc = acc + buf[POOL_HALO - k:POOL_HALO - k + ts, :]
        cnt = jnp.minimum(pos + 1, win).astype(F32)
        pooled = acc / cnt - u
        mixed = jnp.dot(pooled.astype(BF16), w_ref[g], preferred_element_type=F32) * s_ref[g]
        o_ref[:, cols] = mixed.astype(o_ref.dtype)


def _pool_mix(proj, pool_w, pool_scale, batch, seq, col_block):
    n = batch * seq
    groups, gd, _ = pool_w.shape
    width = groups * gd
    ts = _tile(512, seq)
    ns = seq // ts
    hb = ts // POOL_HALO
    kern = functools.partial(_pool_kernel, ts=ts, gd=gd)
    return pl.pallas_call(
        kern,
        grid=(batch, ns),
        in_specs=[pl.BlockSpec((ts, width), lambda b, i: (b * ns + i, col_block)),
                  pl.BlockSpec((POOL_HALO, width),
                               lambda b, i: (jnp.maximum((b * ns + i) * hb - 1, 0), col_block)),
                  pl.BlockSpec((groups, gd, gd), lambda b, i: (0, 0, 0)),
                  pl.BlockSpec((groups, 1, gd), lambda b, i: (0, 0, 0))],
        out_specs=pl.BlockSpec((ts, width), lambda b, i: (b * ns + i, 0)),
        out_shape=jax.ShapeDtypeStruct((n, width), BF16),
        scratch_shapes=[pltpu.VMEM((POOL_HALO + ts, gd), F32)],
        compiler_params=_cparams(("parallel", "arbitrary")),
        name="pool_mix",
    )(proj, proj, pool_w, pool_scale.reshape(groups, 1, gd))


def _gated_kernel(oa_ref, pm_ref, wa_ref, wp_ref, ga_ref, gp_ref, o_ref):
    ya = jnp.dot(oa_ref[...], wa_ref[...], preferred_element_type=F32)
    yp = jnp.dot(pm_ref[...], wp_ref[...], preferred_element_type=F32)
    ga = jax.nn.sigmoid(ga_ref[...].astype(F32))
    gp = jax.nn.sigmoid(gp_ref[...].astype(F32))
    o_ref[...] = (ga * ya + gp * yp).astype(o_ref.dtype)


def _gated_merge(o_attn, p_mixed, w_a, w_p, proj, gate_col0):
    n, ka = o_attn.shape
    _, kp = p_mixed.shape
    d = w_a.shape[1]
    tm, tn = _tile(1024, n), _tile(512, d)
    gb = gate_col0 // tn
    nb = d // tn
    return pl.pallas_call(
        _gated_kernel,
        grid=(n // tm, nb),
        in_specs=[pl.BlockSpec((tm, ka), lambda i, j: (i, 0)),
                  pl.BlockSpec((tm, kp), lambda i, j: (i, 0)),
                  pl.BlockSpec((ka, tn), lambda i, j: (0, j)),
                  pl.BlockSpec((kp, tn), lambda i, j: (0, j)),
                  pl.BlockSpec((tm, tn), lambda i, j: (i, gb + j)),
                  pl.BlockSpec((tm, tn), lambda i, j: (i, gb + nb + j))],
        out_specs=pl.BlockSpec((tm, tn), lambda i, j: (i, j)),
        out_shape=jax.ShapeDtypeStruct((n, d), BF16),
        compiler_params=_cparams(("parallel", "parallel")),
        name="gated_merge",
    )(o_attn, p_mixed, w_a, w_p, proj, proj)


def _split2(x):
    hi = x.astype(BF16)
    lo = (x - hi.astype(F32)).astype(BF16)
    return hi, lo


def _norm_route_kernel(x_ref, g_ref, wr_ref, br_ref, hp_ref, ids_ref, wts_ref, *,
                       n_groups, per_group):
    x = x_ref[...]
    tm, d = x.shape
    ms = jnp.mean(x * x, axis=-1, keepdims=True)
    h = x * lax.rsqrt(ms + RMS_EPS) * g_ref[...]
    hp_ref[...] = h

    h2 = _split2(h)
    w2 = _split2(wr_ref[...])
    logits = br_ref[...] + jnp.zeros((tm, LANES), F32)
    for a in range(2):
        for b in range(2 - a):
            logits = logits + jnp.dot(h2[a], w2[b], preferred_element_type=F32)

    col = lax.broadcasted_iota(I32, (tm, LANES), 1).astype(F32)
    is_grp = col < n_groups
    gl = jnp.where(is_grp, logits, NEG_BIG)
    gmax = jnp.max(gl, axis=-1, keepdims=True)
    g_top = jnp.min(jnp.where(gl == gmax, col, float(LANES)), axis=-1, keepdims=True)
    denom = jnp.sum(jnp.where(is_grp, jnp.exp(gl - gmax), 0.0), axis=-1, keepdims=True)
    p_group = 1.0 / denom
    first = n_groups + g_top * per_group
    in_grp = (col >= first) & (col < first + per_group)
    el = jnp.where(in_grp, logits, NEG_BIG)
    v1 = jnp.max(el, axis=-1, keepdims=True)
    i1 = jnp.min(jnp.where(el == v1, col, float(LANES)), axis=-1, keepdims=True)
    el2 = jnp.where(col == i1, NEG_BIG, el)
    v2 = jnp.max(el2, axis=-1, keepdims=True)
    i2 = jnp.min(jnp.where(el2 == v2, col, float(LANES)), axis=-1, keepdims=True)
    e2 = jnp.exp(v2 - v1)
    w1 = p_group / (1.0 + e2)
    w2 = p_group * e2 / (1.0 + e2)
    ids = jnp.where(col == 0.0, i1 - n_groups, jnp.where(col == 1.0, i2 - n_groups, 0.0))
    ids_ref[...] = ids.astype(I32)
    wts_ref[...] = jnp.where(col == 0.0, w1, jnp.where(col == 1.0, w2, 0.0))


def _norm_route(x, g, w_rg, b_rg, w_re, b_re):
    n, d = x.shape
    n_groups = w_rg.shape[1]
    n_experts = w_re.shape[1]
    assert n_groups + n_experts <= LANES
    pad = LANES - n_groups - n_experts
    wr = jnp.concatenate([w_rg, w_re, jnp.zeros((d, pad), F32)], axis=1)
    br = jnp.concatenate([b_rg, b_re, jnp.zeros((pad,), F32)]).reshape(1, LANES)
    tm = _tile(256, n)
    kern = functools.partial(_norm_route_kernel, n_groups=n_groups,
                             per_group=n_experts // n_groups)
    return pl.pallas_call(
        kern,
        grid=(n // tm,),
        in_specs=[pl.BlockSpec((tm, d), lambda i: (i, 0)),
                  pl.BlockSpec((1, d), lambda i: (0, 0)),
                  pl.BlockSpec((d, LANES), lambda i: (0, 0)),
                  pl.BlockSpec((1, LANES), lambda i: (0, 0))],
        out_specs=[pl.BlockSpec((tm, d), lambda i: (i, 0)),
                   pl.BlockSpec((tm, LANES), lambda i: (i, 0)),
                   pl.BlockSpec((tm, LANES), lambda i: (i, 0))],
        out_shape=[jax.ShapeDtypeStruct((n, d), F32),
                   jax.ShapeDtypeStruct((n, LANES), I32),
                   jax.ShapeDtypeStruct((n, LANES), F32)],
        compiler_params=_cparams(("parallel",)),
        name="norm_route",
    )(x, g.reshape(1, d), wr, br)


def _start_row_copies(idx_ref, base, src_hbm, dst, sem, first, count):
    for r in range(first, first + count):
        pltpu.make_async_copy(src_hbm.at[pl.ds(idx_ref[base + r], 1)], dst.at[pl.ds(r, 1)], sem).start()


def _expert_up_kernel(tok_ref, te_ref, valid_ref, h_hbm, wg_ref, wu_ref, o_ref, xbuf, xb, sem, *,
                      tm, fc):
    t = pl.program_id(0)
    last = pl.num_programs(0) - 1
    slot = t % 2
    valid = valid_ref[t] != 0
    f = o_ref.shape[1]
    nseg = 2 * (f // fc)
    per_seg = tm // nseg

    @pl.when(t == 0)
    def _():
        _start_row_copies(tok_ref, 0, h_hbm, xbuf.at[0], sem.at[0], 0, tm)

    pltpu.make_async_copy(h_hbm.at[pl.ds(0, tm)], xbuf.at[slot], sem.at[slot]).wait()

    def prefetch(seg):
        _start_row_copies(tok_ref, (t + 1) * tm, h_hbm, xbuf.at[1 - slot], sem.at[1 - slot],
                          seg * per_seg, per_seg)

    def ffn(more):
        xb[...] = xbuf[slot].astype(BF16)
        for ci, c in enumerate(range(0, f, fc)):
            if more:
                prefetch(2 * ci)
            g = jnp.dot(xb[...], wg_ref[0, :, c:c + fc], preferred_element_type=F32)
            if more:
                prefetch(2 * ci + 1)
            u = jnp.dot(xb[...], wu_ref[0, :, c:c + fc], preferred_element_type=F32)
            o_ref[:, c:c + fc] = (g * jax.nn.sigmoid(g) * u).astype(o_ref.dtype)

    def skip(more):
        if more:
            for seg in range(nseg):
                prefetch(seg)
        o_ref[...] = jnp.zeros_like(o_ref)

    for more, live in ((True, True), (True, False), (False, True), (False, False)):
        @pl.when(((t < last) == more) & (valid == live))
        def _(more=more, live=live):
            ffn(more) if live else skip(more)


def _expert_up(h, row_tok, w_gate, w_up, tile_e, tile_valid, tm):
    rows = row_tok.shape[0]
    _, d, f = w_gate.shape
    nt = rows // tm
    fc = _tile(256, f)
    return pl.pallas_call(
        functools.partial(_expert_up_kernel, tm=tm, fc=fc),
        grid_spec=pltpu.PrefetchScalarGridSpec(
            num_scalar_prefetch=3,
            grid=(nt,),
            in_specs=[pl.BlockSpec(memory_space=pl.ANY),
                      pl.BlockSpec((1, d, f), lambda t, tok, te, tv: (te[t], 0, 0)),
                      pl.BlockSpec((1, d, f), lambda t, tok, te, tv: (te[t], 0, 0))],
            out_specs=pl.BlockSpec((tm, f), lambda t, tok, te, tv: (t, 0)),
            scratch_shapes=[pltpu.VMEM((2, tm, d), F32), pltpu.VMEM((tm, d), BF16),
                            pltpu.SemaphoreType.DMA((2,))]),
        out_shape=jax.ShapeDtypeStruct((rows, f), BF16),
        compiler_params=_cparams(("arbitrary",)),
        name="expert_up",
    )(row_tok, tile_e, tile_valid, h, w_gate, w_up)


def _expert_down_kernel(te_ref, valid_ref, h_ref, wd_ref, rw_ref, o_ref):
    t = pl.program_id(0)

    @pl.when(valid_ref[t] != 0)
    def _():
        y = jnp.dot(h_ref[...], wd_ref[0], preferred_element_type=F32)
        o_ref[...] = (y * rw_ref[...]).astype(o_ref.dtype)

    @pl.when(valid_ref[t] == 0)
    def _():
        o_ref[...] = jnp.zeros_like(o_ref)


def _expert_down(hact, w_down, row_w, tile_e, tile_valid, tm):
    rows, f = hact.shape
    _, _, d = w_down.shape
    nt = rows // tm
    return pl.pallas_call(
        _expert_down_kernel,
        grid_spec=pltpu.PrefetchScalarGridSpec(
            num_scalar_prefetch=2,
            grid=(nt,),
            in_specs=[pl.BlockSpec((tm, f), lambda t, te, tv: (t, 0)),
                      pl.BlockSpec((1, f, d), lambda t, te, tv: (te[t], 0, 0)),
                      pl.BlockSpec((tm, 1), lambda t, te, tv: (t, 0))],
            out_specs=pl.BlockSpec((tm, d), lambda t, te, tv: (t, 0))),
        out_shape=jax.ShapeDtypeStruct((rows, d), F32),
        compiler_params=_cparams(("arbitrary",)),
        name="expert_down",
    )(tile_e, tile_valid, hact, w_down, row_w.reshape(rows, 1))


COMBINE_ROWS = 8


def _combine_kernel(pos_ref, x_ref, ys_hbm, g_ref, o_ref, buf, sem, *, tm, final_norm):
    i = pl.program_id(0)
    last = pl.num_programs(0) - 1
    slot = i % 2

    def gather(tile, s):
        for k in range(TOP_K):
            for r in range(tm):
                p = pos_ref[TOP_K * (tile * tm + r) + k]
                pltpu.make_async_copy(ys_hbm.at[pl.ds(p, 1)], buf.at[s, k, pl.ds(r, 1)],
                                      sem.at[s, k]).start()

    @pl.when(i == 0)
    def _():
        gather(0, 0)

    for k in range(TOP_K):
        pltpu.make_async_copy(ys_hbm.at[pl.ds(0, tm)], buf.at[slot, k], sem.at[slot, k]).wait()

    def finish():
        for c in range(0, tm, COMBINE_ROWS):
            rows = slice(c, c + COMBINE_ROWS)
            y = x_ref[rows, :]
            for k in range(TOP_K):
                y = y + buf[slot, k, rows, :]
            if final_norm:
                ms = jnp.mean(y * y, axis=-1, keepdims=True)
                y = y * lax.rsqrt(ms + RMS_EPS) * g_ref[...]
            o_ref[rows, :] = y

    @pl.when(i < last)
    def _():
        gather(i + 1, 1 - slot)
        finish()

    @pl.when(i == last)
    def _():
        finish()


def _combine(x1, ys, pos, g, final_norm):
    n, d = x1.shape
    tm = _tile(128, n)
    return pl.pallas_call(
        functools.partial(_combine_kernel, tm=tm, final_norm=final_norm),
        grid_spec=pltpu.PrefetchScalarGridSpec(
            num_scalar_prefetch=1,
            grid=(n // tm,),
            in_specs=[pl.BlockSpec((tm, d), lambda i, pos: (i, 0)),
                      pl.BlockSpec(memory_space=pl.ANY),
                      pl.BlockSpec((1, d), lambda i, pos: (0, 0))],
            out_specs=pl.BlockSpec((tm, d), lambda i, pos: (i, 0)),
            scratch_shapes=[pltpu.VMEM((2, TOP_K, tm, d), F32),
                            pltpu.SemaphoreType.DMA((2, TOP_K))]),
        out_shape=jax.ShapeDtypeStruct((n, d), F32),
        compiler_params=_cparams(("arbitrary",)),
        name="moe_combine",
    )(pos, x1, ys, g.reshape(1, d))


def _dispatch_plan(expert_ids, weights, n_experts, tm):
    n = expert_ids.shape[0]
    a = n * TOP_K
    flat_e = expert_ids.reshape(a)
    onehot = (flat_e[:, None] == jnp.arange(n_experts, dtype=I32)[None, :]).astype(I32)
    incl = jnp.cumsum(onehot, axis=0)
    rank = jnp.sum((incl - onehot) * onehot, axis=1)
    counts = incl[-1]
    padded = (counts + tm - 1) // tm * tm
    pends = jnp.cumsum(padded)
    pstarts = pends - padded
    dest = (pstarts[flat_e] + rank).astype(I32)
    n_tiles = a // tm + n_experts
    rows = n_tiles * tm
    fields = jnp.stack([(jnp.arange(a, dtype=I32) // TOP_K).astype(F32), weights.reshape(a)], axis=1)
    row_fields = jnp.zeros((rows, 2), F32).at[dest].set(fields)
    row_tok = row_fields[:, 0].astype(I32)
    row_w = row_fields[:, 1]
    tile_start = jnp.arange(n_tiles, dtype=I32) * tm
    tile_e = jnp.minimum(jnp.sum((tile_start[:, None] >= pends[None, :]).astype(I32), axis=1),
                         n_experts - 1)
    tile_valid = (tile_start < pends[-1]).astype(I32)
    return row_tok, row_w, dest, tile_e, tile_valid


def kernel(x, norm_mix, w_in, w_o_attn, pool_w, pool_scale, w_o_pool, w_out, norm_ffn,
           w_router_group, b_router_group, w_router_expert, b_router_expert,
           w_gate, w_up, w_down, norm_final):
    batch, seq, d = x.shape
    n = batch * seq
    depth = norm_mix.shape[0]
    sb_width = w_o_attn.shape[1]
    n_heads = sb_width // HEAD_DIM
    groups, gd = pool_w.shape[1], pool_w.shape[2]
    pool_width = groups * gd
    n_experts = w_router_expert.shape[2]
    moe_tm = 256
    assert pool_width == sb_width and (3 * sb_width) % pool_width == 0

    xf = x.reshape(n, d)
    for l in range(depth):
        h = _rmsnorm(xf, norm_mix[l], BF16)
        col_scale = jnp.where(jnp.arange(w_in.shape[2]) < sb_width,
                              math.log2(math.e) / math.sqrt(HEAD_DIM), 1.0)
        w_in_l = (w_in[l] * col_scale[None, :].astype(F32)).astype(BF16)
        proj = _matmul(h, w_in_l, BF16, tm=1024, tn=1024, name="in_proj")
        o_attn = _attention(proj, batch, seq, n_heads)
        p_mixed = _pool_mix(proj, pool_w[l].astype(BF16), pool_scale[l], batch, seq,
                            (3 * sb_width) // pool_width)
        merged = _gated_merge(o_attn, p_mixed, w_o_attn[l].astype(BF16), w_o_pool[l].astype(BF16),
                              proj, 3 * sb_width + pool_width)
        x1 = _matmul(merged, w_out[l].astype(BF16), F32, tm=512, tn=1024, residual=xf,
                     name="out_proj")
        hp, ids_slab, wts_slab = _norm_route(x1, norm_ffn[l], w_router_group[l], b_router_group[l],
                                             w_router_expert[l], b_router_expert[l])
        row_tok, row_w, pos, tile_e, tile_valid = _dispatch_plan(
            ids_slab[:, :TOP_K], wts_slab[:, :TOP_K], n_experts, moe_tm)
        hact = _expert_up(hp, row_tok, w_gate[l].astype(BF16), w_up[l].astype(BF16), tile_e, tile_valid,
                          moe_tm)
        ys = _expert_down(hact, w_down[l].astype(BF16), row_w, tile_e, tile_valid, moe_tm)
        last = l == depth - 1
        xf = _combine(x1, ys, pos, norm_final, final_norm=last)
    return xf.reshape(batch, seq, d)
```

```python
import functools
import math

import jax
import jax.numpy as jnp
from jax import lax
from jax.experimental import pallas as pl
from jax.experimental.pallas import tpu as pltpu

F32 = jnp.float32
BF16 = jnp.bfloat16
U32 = jnp.uint32
I32 = jnp.int32

HEAD_DIM = 128
POOL_SIZES = (2, 4, 8, 16)
POOL_HALO = 16
TOP_K = 2
RMS_EPS = 1e-6
LANES = 128
VMEM_LIMIT = 56 * 1024 * 1024
NEG_BIG = -1e30


def _cparams(sem):
    return pltpu.CompilerParams(dimension_semantics=sem, vmem_limit_bytes=VMEM_LIMIT)


def _tile(pref, dim):
    t = min(pref, dim)
    assert dim % t == 0, (pref, dim)
    return t


def _rmsnorm_kernel(x_ref, g_ref, o_ref):
    x = x_ref[...]
    ms = jnp.mean(x * x, axis=-1, keepdims=True)
    o_ref[...] = (x * lax.rsqrt(ms + RMS_EPS) * g_ref[...]).astype(o_ref.dtype)


def _rmsnorm(x, g, out_dtype):
    n, d = x.shape
    tm = _tile(256, n)
    return pl.pallas_call(
        _rmsnorm_kernel,
        grid=(n // tm,),
        in_specs=[pl.BlockSpec((tm, d), lambda i: (i, 0)),
                  pl.BlockSpec((1, d), lambda i: (0, 0))],
        out_specs=pl.BlockSpec((tm, d), lambda i: (i, 0)),
        out_shape=jax.ShapeDtypeStruct((n, d), out_dtype),
        compiler_params=_cparams(("parallel",)),
        name="rmsnorm",
    )(x, g.reshape(1, d))


def _mm_kernel(a_ref, b_ref, o_ref):
    o_ref[...] = jnp.dot(a_ref[...], b_ref[...], preferred_element_type=F32).astype(o_ref.dtype)


def _mm_res_kernel(a_ref, b_ref, r_ref, o_ref):
    o_ref[...] = (r_ref[...] + jnp.dot(a_ref[...], b_ref[...], preferred_element_type=F32)
                  ).astype(o_ref.dtype)


def _matmul(a, b, out_dtype, *, tm, tn, residual=None, name):
    m, k = a.shape
    _, n = b.shape
    tm, tn = _tile(tm, m), _tile(tn, n)
    in_specs = [pl.BlockSpec((tm, k), lambda i, j: (i, 0)),
                pl.BlockSpec((k, tn), lambda i, j: (0, j))]
    args = [a, b]
    body = _mm_kernel
    if residual is not None:
        in_specs.append(pl.BlockSpec((tm, tn), lambda i, j: (i, j)))
        args.append(residual)
        body = _mm_res_kernel
    return pl.pallas_call(
        body,
        grid=(m // tm, n // tn),
        in_specs=in_specs,
        out_specs=pl.BlockSpec((tm, tn), lambda i, j: (i, j)),
        out_shape=jax.ShapeDtypeStruct((m, n), out_dtype),
        compiler_params=_cparams(("parallel", "parallel")),
        name=name,
    )(*args)


ATTN_SUB = 256
ATTN_HEADS = 2


def _cast_stream_step(step, nsteps, srcs, dsts, inbufs, outbufs, in_sem, out_sem):
    slot = step % 2

    def rows(j):
        return inbufs[j].shape[1]

    def read(j, chunk, sl):
        start = pl.multiple_of(chunk * rows(j), rows(j))
        return pltpu.make_async_copy(srcs[j].at[pl.ds(start, rows(j))], inbufs[j].at[sl],
                                     in_sem.at[j, sl])

    def write(j, chunk, sl):
        start = pl.multiple_of(chunk * rows(j), rows(j))
        return pltpu.make_async_copy(outbufs[j].at[sl], dsts[j].at[pl.ds(start, rows(j))],
                                     out_sem.at[j, sl])

    n = len(srcs)

    @pl.when(step == 0)
    def _():
        for j in range(n):
            read(j, 0, 0).start()

    @pl.when(step + 1 < nsteps)
    def _():
        for j in range(n):
            read(j, step + 1, 1 - slot).start()

    @pl.when(step >= 2)
    def _():
        for j in range(n):
            write(j, step - 2, slot).wait()

    for j in range(n):
        read(j, step, slot).wait()
        outbufs[j][slot] = inbufs[j][slot].astype(outbufs[j].dtype)
        write(j, step, slot).start()

    @pl.when(step == nsteps - 1)
    def _():
        for j in range(n):
            write(j, step, slot).wait()
            if nsteps >= 2:
                write(j, step - 1, 1 - slot).wait()


def _attn_kernel(q_ref, k_ref, v_ref, *rest, tq, sub, heads, n_stream, grid):
    ns = n_stream
    srcs, o_ref, dsts = rest[:ns], rest[ns], rest[ns + 1:2 * ns + 1]
    acc_ref, carry_ref = rest[2 * ns + 1:2 * ns + 3]
    inbufs, outbufs = rest[2 * ns + 3:3 * ns + 3], rest[3 * ns + 3:4 * ns + 3]
    _attn_tile(q_ref, k_ref, v_ref, o_ref, acc_ref, carry_ref, tq=tq, sub=sub, heads=heads)
    if ns:
        in_sem, out_sem = rest[4 * ns + 3:]
        step = (pl.program_id(0) * grid[1] + pl.program_id(1)) * grid[2] + pl.program_id(2)
        _cast_stream_step(step, grid[0] * grid[1] * grid[2], srcs, dsts, inbufs, outbufs,
                          in_sem, out_sem)


def _attn_tile(q_ref, k_ref, v_ref, o_ref, acc_ref, carry_ref, *, tq, sub, heads):
    i = pl.program_id(2)
    nsub = tq // sub
    qi = lax.broadcasted_iota(I32, (sub, sub), 0)
    ki = lax.broadcasted_iota(I32, (sub, sub), 1)
    tri = jnp.where(qi >= ki, 1.0, 0.0).astype(BF16)
    causal = ki < qi

    def attend(rows, start, nkeys, masked, carries, accs):
        nrows = rows.stop - rows.start
        start = pl.multiple_of(start, sub)
        hcols = [slice(h * HEAD_DIM, (h + 1) * HEAD_DIM) for h in range(heads)]
        z = [lax.dot_general(q_ref[rows, hc], k_ref[pl.ds(start, nkeys), hc],
                             (((1,), (1,)), ((), ())), preferred_element_type=F32) for hc in hcols]
        sp = [jnp.maximum(zh, 0.0) + jnp.log2(1.0 + jnp.exp2(-jnp.abs(zh))) for zh in z]
        if masked:
            sp = [jnp.where(causal, s, 0.0) for s in sp]
        chunks = [slice(c, c + sub) for c in range(0, nkeys, sub)]
        spb = [s.astype(BF16) for s in sp]
        cs = [[jnp.dot(s[:, ck], tri, preferred_element_type=F32) for ck in chunks] for s in spb]
        tot = [[jnp.broadcast_to(c[:, 0:1], (nrows, HEAD_DIM)) for c in ch] for ch in cs]
        new_carries, new_accs = [], []
        for h in range(heads):
            carry = carries[h]
            w = [None] * (nkeys // LANES)
            for cb in range(len(chunks) - 1, -1, -1):
                for c0 in range(0, sub, LANES):
                    g0 = cb * sub + c0
                    w[g0 // LANES] = jnp.exp2(z[h][:, g0:g0 + LANES] - cs[h][cb][:, c0:c0 + LANES]
                                              - carry)
                carry = carry + tot[h][cb]
            w = jnp.concatenate(w, axis=1)
            if masked:
                w = jnp.where(causal, w, 0.0)
            new_carries.append(carry)
            new_accs.append(accs[h] + jnp.dot(w.astype(BF16), v_ref[pl.ds(start, nkeys), hcols[h]],
                                              preferred_element_type=F32))
        return new_carries, new_accs

    base = i * tq
    for rb in range(nsub):
        rows = slice(rb * sub, (rb + 1) * sub)
        zero = [jnp.zeros((sub, HEAD_DIM), F32)] * heads
        carries, accs = attend(rows, base + rb * sub, sub, True, zero, zero)
        for cb in range(rb - 1, -1, -1):
            carries, accs = attend(rows, base + cb * sub, sub, False, carries, accs)
        acc_ref[rows, :] = jnp.concatenate(accs, axis=1)
        carry_ref[rows, :] = jnp.concatenate(carries, axis=1)

    def body(n, c):
        full = slice(0, tq)
        hcols = [slice(h * HEAD_DIM, (h + 1) * HEAD_DIM) for h in range(heads)]
        carries, accs = attend(full, (i - 1 - n) * tq, tq, False,
                               [carry_ref[:, hc] for hc in hcols], [acc_ref[:, hc] for hc in hcols])
        acc_ref[...] = jnp.concatenate(accs, axis=1)
        carry_ref[...] = jnp.concatenate(carries, axis=1)
        return c

    lax.fori_loop(0, i, body, 0)
    o_ref[...] = acc_ref[...].astype(o_ref.dtype)


BF16_SUBLANES = 16


def _attention(proj, batch, seq, n_heads, to_bf16=()):
    n = batch * seq
    tq = _tile(512, seq)
    sub = _tile(ATTN_SUB, tq)
    nq = seq // tq
    heads = _tile(ATTN_HEADS, n_heads)
    ng = n_heads // heads
    width = heads * HEAD_DIM
    grid = (batch, ng, nq)
    nsteps = batch * ng * nq
    ns = len(to_bf16)
    chunk_rows = [a.shape[0] // nsteps for a in to_bf16]
    for a, r in zip(to_bf16, chunk_rows):
        assert a.shape[0] == r * nsteps and r % BF16_SUBLANES == 0, (a.shape, nsteps)
    any_spec = pl.BlockSpec(memory_space=pl.ANY)
    kern = functools.partial(_attn_kernel, tq=tq, sub=sub, heads=heads, n_stream=ns, grid=grid)
    outs = pl.pallas_call(
        kern,
        grid=grid,
        in_specs=[pl.BlockSpec((tq, width), lambda b, g, i: (b * nq + i, g)),
                  pl.BlockSpec((seq, width), lambda b, g, i: (b, ng + g)),
                  pl.BlockSpec((seq, width), lambda b, g, i: (b, 2 * ng + g))] + [any_spec] * ns,
        out_specs=[pl.BlockSpec((tq, width), lambda b, g, i: (b * nq + i, g))] + [any_spec] * ns,
        out_shape=[jax.ShapeDtypeStruct((n, n_heads * HEAD_DIM), BF16)]
        + [jax.ShapeDtypeStruct(a.shape, BF16) for a in to_bf16],
        scratch_shapes=[pltpu.VMEM((tq, width), F32), pltpu.VMEM((tq, width), F32)]
        + [pltpu.VMEM((2, r, a.shape[1]), F32) for a, r in zip(to_bf16, chunk_rows)]
        + [pltpu.VMEM((2, r, a.shape[1]), BF16) for a, r in zip(to_bf16, chunk_rows)]
        + ([pltpu.SemaphoreType.DMA((ns, 2)), pltpu.SemaphoreType.DMA((ns, 2))] if ns else []),
        compiler_params=_cparams(("arbitrary", "arbitrary", "arbitrary")),
        name="stickbreak_attn",
    )(proj, proj, proj, *to_bf16)
    return outs[0], tuple(outs[1:])


def _pool_kernel(u_ref, halo_ref, w_ref, s_ref, o_ref, buf, *, ts, gd):
    i = pl.program_id(1)
    pos = i * ts + lax.broadcasted_iota(I32, (ts, gd), 0)
    for g, win in enumerate(POOL_SIZES):
        cols = slice(g * gd, (g + 1) * gd)
        u = u_ref[:, cols].astype(F32)
        halo = halo_ref[:, cols].astype(F32)
        buf[0:POOL_HALO, :] = jnp.where(i > 0, halo, 0.0)
        buf[POOL_HALO:POOL_HALO + ts, :] = u
        acc = u
        for k in range(1, win):
            acc = acc + buf[POOL_HALO - k:POOL_HALO - k + ts, :]
        cnt = jnp.minimum(pos + 1, win).astype(F32)
        pooled = acc / cnt - u
        mixed = jnp.dot(pooled.astype(BF16), w_ref[g], preferred_element_type=F32) * s_ref[g]
        o_ref[:, cols] = mixed.astype(o_ref.dtype)


def _pool_mix(proj, pool_w, pool_scale, batch, seq, col_block):
    n = batch * seq
    groups, gd, _ = pool_w.shape
    width = groups * gd
    ts = _tile(512, seq)
    ns = seq // ts
    hb = ts // POOL_HALO
    kern = functools.partial(_pool_kernel, ts=ts, gd=gd)
    return pl.pallas_call(
        kern,
        grid=(batch, ns),
        in_specs=[pl.BlockSpec((ts, width), lambda b, i: (b * ns + i, col_block)),
                  pl.BlockSpec((POOL_HALO, width),
                               lambda b, i: (jnp.maximum((b * ns + i) * hb - 1, 0), col_block)),
                  pl.BlockSpec((groups, gd, gd), lambda b, i: (0, 0, 0)),
                  pl.BlockSpec((groups, 1, gd), lambda b, i: (0, 0, 0))],
        out_specs=pl.BlockSpec((ts, width), lambda b, i: (b * ns + i, 0)),
        out_shape=jax.ShapeDtypeStruct((n, width), BF16),
        scratch_shapes=[pltpu.VMEM((POOL_HALO + ts, gd), F32)],
        compiler_params=_cparams(("parallel", "arbitrary")),
        name="pool_mix",
    )(proj, proj, pool_w, pool_scale.reshape(groups, 1, gd))


def _gated_kernel(oa_ref, pm_ref, wa_ref, wp_ref, ga_ref, gp_ref, o_ref):
    ya = jnp.dot(oa_ref[...], wa_ref[...], preferred_element_type=F32)
    yp = jnp.dot(pm_ref[...], wp_ref[...], preferred_element_type=F32)
    ga = jax.nn.sigmoid(ga_ref[...].astype(F32))
    gp = jax.nn.sigmoid(gp_ref[...].astype(F32))
    o_ref[...] = (ga * ya + gp * yp).astype(o_ref.dtype)


def _gated_merge(o_attn, p_mixed, w_a, w_p, proj, gate_col0):
    n, ka = o_attn.shape
    _, kp = p_mixed.shape
    d = w_a.shape[1]
    tm, tn = _tile(1024, n), _tile(512, d)
    gb = gate_col0 // tn
    nb = d // tn
    return pl.pallas_call(
        _gated_kernel,
        grid=(n // tm, nb),
        in_specs=[pl.BlockSpec((tm, ka), lambda i, j: (i, 0)),
                  pl.BlockSpec((tm, kp), lambda i, j: (i, 0)),
                  pl.BlockSpec((ka, tn), lambda i, j: (0, j)),
                  pl.BlockSpec((kp, tn), lambda i, j: (0, j)),
                  pl.BlockSpec((tm, tn), lambda i, j: (i, gb + j)),
                  pl.BlockSpec((tm, tn), lambda i, j: (i, gb + nb + j))],
        out_specs=pl.BlockSpec((tm, tn), lambda i, j: (i, j)),
        out_shape=jax.ShapeDtypeStruct((n, d), BF16),
        compiler_params=_cparams(("parallel", "parallel")),
        name="gated_merge",
    )(o_attn, p_mixed, w_a, w_p, proj, proj)


def _split2(x):
    hi = x.astype(BF16)
    lo = (x - hi.astype(F32)).astype(BF16)
    return hi, lo


def _norm_route_kernel(x_ref, g_ref, wr_ref, br_ref, hp_ref, ids_ref, wts_ref, *,
                       n_groups, per_group):
    x = x_ref[...]
    tm, d = x.shape
    ms = jnp.mean(x * x, axis=-1, keepdims=True)
    h = x * lax.rsqrt(ms + RMS_EPS) * g_ref[...]
    hp_ref[...] = h

    h2 = _split2(h)
    w2 = _split2(wr_ref[...])
    logits = br_ref[...] + jnp.zeros((tm, LANES), F32)
    for a in range(2):
        for b in range(2 - a):
            logits = logits + jnp.dot(h2[a], w2[b], preferred_element_type=F32)

    col = lax.broadcasted_iota(I32, (tm, LANES), 1).astype(F32)
    is_grp = col < n_groups
    gl = jnp.where(is_grp, logits, NEG_BIG)
    gmax = jnp.max(gl, axis=-1, keepdims=True)
    g_top = jnp.min(jnp.where(gl == gmax, col, float(LANES)), axis=-1, keepdims=True)
    denom = jnp.sum(jnp.where(is_grp, jnp.exp(gl - gmax), 0.0), axis=-1, keepdims=True)
    p_group = 1.0 / denom
    first = n_groups + g_top * per_group
    in_grp = (col >= first) & (col < first + per_group)
    el = jnp.where(in_grp, logits, NEG_BIG)
    v1 = jnp.max(el, axis=-1, keepdims=True)
    i1 = jnp.min(jnp.where(el == v1, col, float(LANES)), axis=-1, keepdims=True)
    el2 = jnp.where(col == i1, NEG_BIG, el)
    v2 = jnp.max(el2, axis=-1, keepdims=True)
    i2 = jnp.min(jnp.where(el2 == v2, col, float(LANES)), axis=-1, keepdims=True)
    e2 = jnp.exp(v2 - v1)
    w1 = p_group / (1.0 + e2)
    w2 = p_group * e2 / (1.0 + e2)
    ids = jnp.where(col == 0.0, i1 - n_groups, jnp.where(col == 1.0, i2 - n_groups, 0.0))
    ids_ref[...] = ids.astype(I32)
    wts_ref[...] = jnp.where(col == 0.0, w1, jnp.where(col == 1.0, w2, 0.0))


def _norm_route(x, g, w_rg, b_rg, w_re, b_re):
    n, d = x.shape
    n_groups = w_rg.shape[1]
    n_experts = w_re.shape[1]
    assert n_groups + n_experts <= LANES
    pad = LANES - n_groups - n_experts
    wr = jnp.concatenate([w_rg, w_re, jnp.zeros((d, pad), F32)], axis=1)
    br = jnp.concatenate([b_rg, b_re, jnp.zeros((pad,), F32)]).reshape(1, LANES)
    tm = _tile(256, n)
    kern = functools.partial(_norm_route_kernel, n_groups=n_groups,
                             per_group=n_experts // n_groups)
    return pl.pallas_call(
        kern,
        grid=(n // tm,),
        in_specs=[pl.BlockSpec((tm, d), lambda i: (i, 0)),
                  pl.BlockSpec((1, d), lambda i: (0, 0)),
                  pl.BlockSpec((d, LANES), lambda i: (0, 0)),
                  pl.BlockSpec((1, LANES), lambda i: (0, 0))],
        out_specs=[pl.BlockSpec((tm, d), lambda i: (i, 0)),
                   pl.BlockSpec((tm, LANES), lambda i: (i, 0)),
                   pl.BlockSpec((tm, LANES), lambda i: (i, 0))],
        out_shape=[jax.ShapeDtypeStruct((n, d), F32),
                   jax.ShapeDtypeStruct((n, LANES), I32),
                   jax.ShapeDtypeStruct((n, LANES), F32)],
        compiler_params=_cparams(("parallel",)),
        name="norm_route",
    )(x, g.reshape(1, d), wr, br)


def _start_row_copies(idx_ref, base, src_hbm, dst, sem, first, count):
    for r in range(first, first + count):
        pltpu.make_async_copy(src_hbm.at[pl.ds(idx_ref[base + r], 1)], dst.at[pl.ds(r, 1)], sem).start()


def _expert_up_kernel(tok_ref, te_ref, valid_ref, h_hbm, wg_ref, wu_ref, o_ref, xbuf, xb, sem, *,
                      tm, fc):
    t = pl.program_id(0)
    last = pl.num_programs(0) - 1
    slot = t % 2
    valid = valid_ref[t] != 0
    f = o_ref.shape[1]
    nseg = 2 * (f // fc)
    per_seg = tm // nseg

    @pl.when(t == 0)
    def _():
        _start_row_copies(tok_ref, 0, h_hbm, xbuf.at[0], sem.at[0], 0, tm)

    pltpu.make_async_copy(h_hbm.at[pl.ds(0, tm)], xbuf.at[slot], sem.at[slot]).wait()

    def prefetch(seg):
        _start_row_copies(tok_ref, (t + 1) * tm, h_hbm, xbuf.at[1 - slot], sem.at[1 - slot],
                          seg * per_seg, per_seg)

    def ffn(more):
        xb[...] = xbuf[slot].astype(BF16)
        for ci, c in enumerate(range(0, f, fc)):
            if more:
                prefetch(2 * ci)
            g = jnp.dot(xb[...], wg_ref[0, :, c:c + fc], preferred_element_type=F32)
            if more:
                prefetch(2 * ci + 1)
            u = jnp.dot(xb[...], wu_ref[0, :, c:c + fc], preferred_element_type=F32)
            o_ref[:, c:c + fc] = (g * jax.nn.sigmoid(g) * u).astype(o_ref.dtype)

    def skip(more):
        if more:
            for seg in range(nseg):
                prefetch(seg)
        o_ref[...] = jnp.zeros_like(o_ref)

    for more, live in ((True, True), (True, False), (False, True), (False, False)):
        @pl.when(((t < last) == more) & (valid == live))
        def _(more=more, live=live):
            ffn(more) if live else skip(more)


def _expert_up(h, row_tok, w_gate, w_up, tile_e, tile_valid, tm):
    rows = row_tok.shape[0]
    _, d, f = w_gate.shape
    nt = rows // tm
    fc = _tile(256, f)
    return pl.pallas_call(
        functools.partial(_expert_up_kernel, tm=tm, fc=fc),
        grid_spec=pltpu.PrefetchScalarGridSpec(
            num_scalar_prefetch=3,
            grid=(nt,),
            in_specs=[pl.BlockSpec(memory_space=pl.ANY),
                      pl.BlockSpec((1, d, f), lambda t, tok, te, tv: (te[t], 0, 0)),
                      pl.BlockSpec((1, d, f), lambda t, tok, te, tv: (te[t], 0, 0))],
            out_specs=pl.BlockSpec((tm, f), lambda t, tok, te, tv: (t, 0)),
            scratch_shapes=[pltpu.VMEM((2, tm, d), F32), pltpu.VMEM((tm, d), BF16),
                            pltpu.SemaphoreType.DMA((2,))]),
        out_shape=jax.ShapeDtypeStruct((rows, f), BF16),
        compiler_params=_cparams(("arbitrary",)),
        name="expert_up",
    )(row_tok, tile_e, tile_valid, h, w_gate, w_up)


def _expert_down_kernel(te_ref, valid_ref, h_ref, wd_ref, rw_ref, o_ref):
    t = pl.program_id(0)

    @pl.when(valid_ref[t] != 0)
    def _():
        y = jnp.dot(h_ref[...], wd_ref[0], preferred_element_type=F32)
        o_ref[...] = (y * rw_ref[...]).astype(o_ref.dtype)

    @pl.when(valid_ref[t] == 0)
    def _():
        o_ref[...] = jnp.zeros_like(o_ref)


def _expert_down(hact, w_down, row_w, tile_e, tile_valid, tm):
    rows, f = hact.shape
    _, _, d = w_down.shape
    nt = rows // tm
    return pl.pallas_call(
        _expert_down_kernel,
        grid_spec=pltpu.PrefetchScalarGridSpec(
            num_scalar_prefetch=2,
            grid=(nt,),
            in_specs=[pl.BlockSpec((tm, f), lambda t, te, tv: (t, 0)),
                      pl.BlockSpec((1, f, d), lambda t, te, tv: (te[t], 0, 0)),
                      pl.BlockSpec((tm, 1), lambda t, te, tv: (t, 0))],
            out_specs=pl.BlockSpec((tm, d), lambda t, te, tv: (t, 0))),
        out_shape=jax.ShapeDtypeStruct((rows, d), F32),
        compiler_params=_cparams(("arbitrary",)),
        name="expert_down",
    )(tile_e, tile_valid, hact, w_down, row_w.reshape(rows, 1))


COMBINE_ROWS = 8


def _combine_kernel(pos_ref, x_ref, ys_hbm, g_ref, o_ref, buf, sem, *, tm, final_norm):
    i = pl.program_id(0)
    last = pl.num_programs(0) - 1
    slot = i % 2

    def gather(tile, s):
        for k in range(TOP_K):
            for r in range(tm):
                p = pos_ref[TOP_K * (tile * tm + r) + k]
                pltpu.make_async_copy(ys_hbm.at[pl.ds(p, 1)], buf.at[s, k, pl.ds(r, 1)],
                                      sem.at[s, k]).start()

    @pl.when(i == 0)
    def _():
        gather(0, 0)

    for k in range(TOP_K):
        pltpu.make_async_copy(ys_hbm.at[pl.ds(0, tm)], buf.at[slot, k], sem.at[slot, k]).wait()

    def finish():
        for c in range(0, tm, COMBINE_ROWS):
            rows = slice(c, c + COMBINE_ROWS)
            y = x_ref[rows, :]
            for k in range(TOP_K):
                y = y + buf[slot, k, rows, :]
            if final_norm:
                ms = jnp.mean(y * y, axis=-1, keepdims=True)
                y = y * lax.rsqrt(ms + RMS_EPS) * g_ref[...]
            o_ref[rows, :] = y

    @pl.when(i < last)
    def _():
        gather(i + 1, 1 - slot)
        finish()

    @pl.when(i == last)
    def _():
        finish()


def _combine(x1, ys, pos, g, final_norm):
    n, d = x1.shape
    tm = _tile(128, n)
    return pl.pallas_call(
        functools.partial(_combine_kernel, tm=tm, final_norm=final_norm),
        grid_spec=pltpu.PrefetchScalarGridSpec(
            num_scalar_prefetch=1,
            grid=(n // tm,),
            in_specs=[pl.BlockSpec((tm, d), lambda i, pos: (i, 0)),
                      pl.BlockSpec(memory_space=pl.ANY),
                      pl.BlockSpec((1, d), lambda i, pos: (0, 0))],
            out_specs=pl.BlockSpec((tm, d), lambda i, pos: (i, 0)),
            scratch_shapes=[pltpu.VMEM((2, TOP_K, tm, d), F32),
                            pltpu.SemaphoreType.DMA((2, TOP_K))]),
        out_shape=jax.ShapeDtypeStruct((n, d), F32),
        compiler_params=_cparams(("arbitrary",)),
        name="moe_combine",
    )(pos, x1, ys, g.reshape(1, d))


def _dispatch_plan(expert_ids, weights, n_experts, tm):
    n = expert_ids.shape[0]
    a = n * TOP_K
    flat_e = expert_ids.reshape(a)
    onehot = (flat_e[:, None] == jnp.arange(n_experts, dtype=I32)[None, :]).astype(I32)
    incl = jnp.cumsum(onehot, axis=0)
    rank = jnp.sum((incl - onehot) * onehot, axis=1)
    counts = incl[-1]
    padded = (counts + tm - 1) // tm * tm
    pends = jnp.cumsum(padded)
    pstarts = pends - padded
    dest = (pstarts[flat_e] + rank).astype(I32)
    n_tiles = a // tm + n_experts
    rows = n_tiles * tm
    fields = jnp.stack([(jnp.arange(a, dtype=I32) // TOP_K).astype(F32), weights.reshape(a)], axis=1)
    row_fields = jnp.zeros((rows, 2), F32).at[dest].set(fields)
    row_tok = row_fields[:, 0].astype(I32)
    row_w = row_fields[:, 1]
    tile_start = jnp.arange(n_tiles, dtype=I32) * tm
    tile_e = jnp.minimum(jnp.sum((tile_start[:, None] >= pends[None, :]).astype(I32), axis=1),
                         n_experts - 1)
    tile_valid = (tile_start < pends[-1]).astype(I32)
    return row_tok, row_w, dest, tile_e, tile_valid


def kernel(x, norm_mix, w_in, w_o_attn, pool_w, pool_scale, w_o_pool, w_out, norm_ffn,
           w_router_group, b_router_group, w_router_expert, b_router_expert,
           w_gate, w_up, w_down, norm_final):
    batch, seq, d = x.shape
    n = batch * seq
    depth = norm_mix.shape[0]
    sb_width = w_o_attn.shape[1]
    n_heads = sb_width // HEAD_DIM
    groups, gd = pool_w.shape[1], pool_w.shape[2]
    pool_width = groups * gd
    n_experts = w_router_expert.shape[2]
    moe_tm = 256
    assert pool_width == sb_width and (3 * sb_width) % pool_width == 0

    xf = x.reshape(n, d)
    for l in range(depth):
        h = _rmsnorm(xf, norm_mix[l], BF16)
        col_scale = jnp.where(jnp.arange(w_in.shape[2]) < sb_width,
                              math.log2(math.e) / math.sqrt(HEAD_DIM), 1.0)
        w_in_l = (w_in[l] * col_scale[None, :].astype(F32)).astype(BF16)
        proj = _matmul(h, w_in_l, BF16, tm=1024, tn=1024, name="in_proj")
        d_expert = w_gate.shape[3]
        o_attn, (wg_b, wu_b, wd_b) = _attention(
            proj, batch, seq, n_heads,
            to_bf16=(w_gate[l].reshape(n_experts * d, d_expert), w_up[l].reshape(n_experts * d, d_expert),
                     w_down[l].reshape(n_experts * d_expert, d)))
        wg_b = wg_b.reshape(n_experts, d, d_expert)
        wu_b = wu_b.reshape(n_experts, d, d_expert)
        wd_b = wd_b.reshape(n_experts, d_expert, d)
        p_mixed = _pool_mix(proj, pool_w[l].astype(BF16), pool_scale[l], batch, seq,
                            (3 * sb_width) // pool_width)
        merged = _gated_merge(o_attn, p_mixed, w_o_attn[l].astype(BF16), w_o_pool[l].astype(BF16),
                              proj, 3 * sb_width + pool_width)
        x1 = _matmul(merged, w_out[l].astype(BF16), F32, tm=512, tn=1024, residual=xf,
                     name="out_proj")
        hp, ids_slab, wts_slab = _norm_route(x1, norm_ffn[l], w_router_group[l], b_router_group[l],
                                             w_router_expert[l], b_router_expert[l])
        row_tok, row_w, pos, tile_e, tile_valid = _dispatch_plan(
            ids_slab[:, :TOP_K], wts_slab[:, :TOP_K], n_experts, moe_tm)
        hact = _expert_up(hp, row_tok, wg_b, wu_b, tile_e, tile_valid, moe_tm)
        ys = _expert_down(hact, wd_b, row_w, tile_e, tile_valid, moe_tm)
        last = l == depth - 1
        xf = _combine(x1, ys, pos, norm_final, final_norm=last)
    return xf.reshape(batch, seq, d)
```

```python
import functools
import math

import jax
import jax.numpy as jnp
from jax import lax
from jax.experimental import pallas as pl
from jax.experimental.pallas import tpu as pltpu

F32 = jnp.float32
BF16 = jnp.bfloat16
U32 = jnp.uint32
I32 = jnp.int32

HEAD_DIM = 128
POOL_SIZES = (2, 4, 8, 16)
POOL_HALO = 16
TOP_K = 2
RMS_EPS = 1e-6
LANES = 128
VMEM_LIMIT = 56 * 1024 * 1024
NEG_BIG = -1e30


def _cparams(sem):
    return pltpu.CompilerParams(dimension_semantics=sem, vmem_limit_bytes=VMEM_LIMIT)


def _tile(pref, dim):
    t = min(pref, dim)
    assert dim % t == 0, (pref, dim)
    return t


def _rmsnorm_kernel(x_ref, g_ref, o_ref):
    x = x_ref[...]
    ms = jnp.mean(x * x, axis=-1, keepdims=True)
    o_ref[...] = (x * lax.rsqrt(ms + RMS_EPS) * g_ref[...]).astype(o_ref.dtype)


def _rmsnorm(x, g, out_dtype):
    n, d = x.shape
    tm = _tile(256, n)
    return pl.pallas_call(
        _rmsnorm_kernel,
        grid=(n // tm,),
        in_specs=[pl.BlockSpec((tm, d), lambda i: (i, 0)),
                  pl.BlockSpec((1, d), lambda i: (0, 0))],
        out_specs=pl.BlockSpec((tm, d), lambda i: (i, 0)),
        out_shape=jax.ShapeDtypeStruct((n, d), out_dtype),
        compiler_params=_cparams(("parallel",)),
        name="rmsnorm",
    )(x, g.reshape(1, d))


def _mm_kernel(a_ref, b_ref, o_ref):
    o_ref[...] = jnp.dot(a_ref[...], b_ref[...], preferred_element_type=F32).astype(o_ref.dtype)


def _mm_res_kernel(a_ref, b_ref, r_ref, o_ref):
    o_ref[...] = (r_ref[...] + jnp.dot(a_ref[...], b_ref[...], preferred_element_type=F32)
                  ).astype(o_ref.dtype)


def _matmul(a, b, out_dtype, *, tm, tn, residual=None, name):
    m, k = a.shape
    _, n = b.shape
    tm, tn = _tile(tm, m), _tile(tn, n)
    in_specs = [pl.BlockSpec((tm, k), lambda i, j: (i, 0)),
                pl.BlockSpec((k, tn), lambda i, j: (0, j))]
    args = [a, b]
    body = _mm_kernel
    if residual is not None:
        in_specs.append(pl.BlockSpec((tm, tn), lambda i, j: (i, j)))
        args.append(residual)
        body = _mm_res_kernel
    return pl.pallas_call(
        body,
        grid=(m // tm, n // tn),
        in_specs=in_specs,
        out_specs=pl.BlockSpec((tm, tn), lambda i, j: (i, j)),
        out_shape=jax.ShapeDtypeStruct((m, n), out_dtype),
        compiler_params=_cparams(("parallel", "parallel")),
        name=name,
    )(*args)


ATTN_SUB = 256
ATTN_HEADS = 2


def _cast_stream_step(step, total_steps, srcs, dsts, inbufs, outbufs, in_sem, out_sem):
    slot = step % 2

    def rows(j):
        return inbufs[j].shape[1]

    def read(j, chunk, sl):
        start = pl.multiple_of(chunk * rows(j), rows(j))
        return pltpu.make_async_copy(srcs[j].at[pl.ds(start, rows(j))], inbufs[j].at[sl],
                                     in_sem.at[j, sl])

    def write(j, chunk, sl):
        start = pl.multiple_of(chunk * rows(j), rows(j))
        return pltpu.make_async_copy(outbufs[j].at[sl], dsts[j].at[pl.ds(start, rows(j))],
                                     out_sem.at[j, sl])

    by_chunks = {}
    for j in range(len(srcs)):
        by_chunks.setdefault(srcs[j].shape[0] // rows(j), []).append(j)

    for nchunks, js in by_chunks.items():
        assert 1 <= nchunks <= total_steps

        @pl.when(step == 0)
        def _(js=js):
            for j in js:
                read(j, 0, 0).start()

        @pl.when(step + 1 < nchunks)
        def _(js=js):
            for j in js:
                read(j, step + 1, 1 - slot).start()

        @pl.when((step >= 2) & (step < nchunks))
        def _(js=js):
            for j in js:
                write(j, step - 2, slot).wait()

        @pl.when(step < nchunks)
        def _(js=js):
            for j in js:
                read(j, step, slot).wait()
                outbufs[j][slot] = inbufs[j][slot].astype(outbufs[j].dtype)
                write(j, step, slot).start()

        @pl.when(step == nchunks - 1)
        def _(js=js, nchunks=nchunks):
            for j in js:
                write(j, step, slot).wait()
                if nchunks >= 2:
                    write(j, step - 1, 1 - slot).wait()


def _attn_kernel(q_ref, k_ref, v_ref, *rest, tq, sub, heads, n_stream, grid):
    ns = n_stream
    srcs, o_ref, dsts = rest[:ns], rest[ns], rest[ns + 1:2 * ns + 1]
    acc_ref, carry_ref = rest[2 * ns + 1:2 * ns + 3]
    inbufs, outbufs = rest[2 * ns + 3:3 * ns + 3], rest[3 * ns + 3:4 * ns + 3]
    _attn_tile(q_ref, k_ref, v_ref, o_ref, acc_ref, carry_ref, tq=tq, sub=sub, heads=heads)
    if ns:
        in_sem, out_sem = rest[4 * ns + 3:]
        step = (pl.program_id(0) * grid[1] + pl.program_id(1)) * grid[2] + pl.program_id(2)
        _cast_stream_step(step, grid[0] * grid[1] * grid[2], srcs, dsts, inbufs, outbufs,
                          in_sem, out_sem)


def _attn_tile(q_ref, k_ref, v_ref, o_ref, acc_ref, carry_ref, *, tq, sub, heads):
    i = pl.program_id(2)
    nsub = tq // sub
    qi = lax.broadcasted_iota(I32, (sub, sub), 0)
    ki = lax.broadcasted_iota(I32, (sub, sub), 1)
    tri = jnp.where(qi >= ki, 1.0, 0.0).astype(BF16)
    causal = ki < qi

    def attend(rows, start, nkeys, masked, carries, accs):
        nrows = rows.stop - rows.start
        start = pl.multiple_of(start, sub)
        hcols = [slice(h * HEAD_DIM, (h + 1) * HEAD_DIM) for h in range(heads)]
        z = [lax.dot_general(q_ref[rows, hc], k_ref[pl.ds(start, nkeys), hc],
                             (((1,), (1,)), ((), ())), preferred_element_type=F32) for hc in hcols]
        sp = [jnp.maximum(zh, 0.0) + jnp.log2(1.0 + jnp.exp2(-jnp.abs(zh))) for zh in z]
        if masked:
            sp = [jnp.where(causal, s, 0.0) for s in sp]
        chunks = [slice(c, c + sub) for c in range(0, nkeys, sub)]
        spb = [s.astype(BF16) for s in sp]
        cs = [[jnp.dot(s[:, ck], tri, preferred_element_type=F32) for ck in chunks] for s in spb]
        tot = [[jnp.broadcast_to(c[:, 0:1], (nrows, HEAD_DIM)) for c in ch] for ch in cs]
        new_carries, new_accs = [], []
        for h in range(heads):
            carry = carries[h]
            w = [None] * (nkeys // LANES)
            for cb in range(len(chunks) - 1, -1, -1):
                for c0 in range(0, sub, LANES):
                    g0 = cb * sub + c0
                    w[g0 // LANES] = jnp.exp2(z[h][:, g0:g0 + LANES] - cs[h][cb][:, c0:c0 + LANES]
                                              - carry)
                carry = carry + tot[h][cb]
            w = jnp.concatenate(w, axis=1)
            if masked:
                w = jnp.where(causal, w, 0.0)
            new_carries.append(carry)
            new_accs.append(accs[h] + jnp.dot(w.astype(BF16), v_ref[pl.ds(start, nkeys), hcols[h]],
                                              preferred_element_type=F32))
        return new_carries, new_accs

    base = i * tq
    for rb in range(nsub):
        rows = slice(rb * sub, (rb + 1) * sub)
        zero = [jnp.zeros((sub, HEAD_DIM), F32)] * heads
        carries, accs = attend(rows, base + rb * sub, sub, True, zero, zero)
        for cb in range(rb - 1, -1, -1):
            carries, accs = attend(rows, base + cb * sub, sub, False, carries, accs)
        acc_ref[rows, :] = jnp.concatenate(accs, axis=1)
        carry_ref[rows, :] = jnp.concatenate(carries, axis=1)

    def body(n, c):
        full = slice(0, tq)
        hcols = [slice(h * HEAD_DIM, (h + 1) * HEAD_DIM) for h in range(heads)]
        carries, accs = attend(full, (i - 1 - n) * tq, tq, False,
                               [carry_ref[:, hc] for hc in hcols], [acc_ref[:, hc] for hc in hcols])
        acc_ref[...] = jnp.concatenate(accs, axis=1)
        carry_ref[...] = jnp.concatenate(carries, axis=1)
        return c

    lax.fori_loop(0, i, body, 0)
    o_ref[...] = acc_ref[...].astype(o_ref.dtype)


BF16_SUBLANES = 16


def _attention(proj, batch, seq, n_heads, to_bf16=()):
    n = batch * seq
    tq = _tile(512, seq)
    sub = _tile(ATTN_SUB, tq)
    nq = seq // tq
    heads = _tile(ATTN_HEADS, n_heads)
    ng = n_heads // heads
    width = heads * HEAD_DIM
    grid = (batch, ng, nq)
    nsteps = batch * ng * nq
    ns = len(to_bf16)
    chunk_rows = [max(BF16_SUBLANES, a.shape[0] // nsteps) for a in to_bf16]
    for a, r in zip(to_bf16, chunk_rows):
        assert a.shape[0] % r == 0 and r % BF16_SUBLANES == 0 and a.shape[0] // r <= nsteps, (
            a.shape, nsteps)
    any_spec = pl.BlockSpec(memory_space=pl.ANY)
    kern = functools.partial(_attn_kernel, tq=tq, sub=sub, heads=heads, n_stream=ns, grid=grid)
    outs = pl.pallas_call(
        kern,
        grid=grid,
        in_specs=[pl.BlockSpec((tq, width), lambda b, g, i: (b * nq + i, g)),
                  pl.BlockSpec((seq, width), lambda b, g, i: (b, ng + g)),
                  pl.BlockSpec((seq, width), lambda b, g, i: (b, 2 * ng + g))] + [any_spec] * ns,
        out_specs=[pl.BlockSpec((tq, width), lambda b, g, i: (b * nq + i, g))] + [any_spec] * ns,
        out_shape=[jax.ShapeDtypeStruct((n, n_heads * HEAD_DIM), BF16)]
        + [jax.ShapeDtypeStruct(a.shape, BF16) for a in to_bf16],
        scratch_shapes=[pltpu.VMEM((tq, width), F32), pltpu.VMEM((tq, width), F32)]
        + [pltpu.VMEM((2, r, a.shape[1]), F32) for a, r in zip(to_bf16, chunk_rows)]
        + [pltpu.VMEM((2, r, a.shape[1]), BF16) for a, r in zip(to_bf16, chunk_rows)]
        + ([pltpu.SemaphoreType.DMA((ns, 2)), pltpu.SemaphoreType.DMA((ns, 2))] if ns else []),
        compiler_params=_cparams(("arbitrary", "arbitrary", "arbitrary")),
        name="stickbreak_attn",
    )(proj, proj, proj, *to_bf16)
    return outs[0], tuple(outs[1:])


def _pool_kernel(u_ref, halo_ref, w_ref, s_ref, o_ref, buf, *, ts, gd):
    i = pl.program_id(1)
    pos = i * ts + lax.broadcasted_iota(I32, (ts, gd), 0)
    for g, win in enumerate(POOL_SIZES):
        cols = slice(g * gd, (g + 1) * gd)
        u = u_ref[:, cols].astype(F32)
        halo = halo_ref[:, cols].astype(F32)
        buf[0:POOL_HALO, :] = jnp.where(i > 0, halo, 0.0)
        buf[POOL_HALO:POOL_HALO + ts, :] = u
        acc = u
        for k in range(1, win):
            acc = acc + buf[POOL_HALO - k:POOL_HALO - k + ts, :]
        cnt = jnp.minimum(pos + 1, win).astype(F32)
        pooled = acc / cnt - u
        mixed = jnp.dot(pooled.astype(BF16), w_ref[g], preferred_element_type=F32) * s_ref[g]
        o_ref[:, cols] = mixed.astype(o_ref.dtype)


def _pool_mix(proj, pool_w, pool_scale, batch, seq, col_block):
    n = batch * seq
    groups, gd, _ = pool_w.shape
    width = groups * gd
    ts = _tile(512, seq)
    ns = seq // ts
    hb = ts // POOL_HALO
    kern = functools.partial(_pool_kernel, ts=ts, gd=gd)
    return pl.pallas_call(
        kern,
        grid=(batch, ns),
        in_specs=[pl.BlockSpec((ts, width), lambda b, i: (b * ns + i, col_block)),
                  pl.BlockSpec((POOL_HALO, width),
                               lambda b, i: (jnp.maximum((b * ns + i) * hb - 1, 0), col_block)),
                  pl.BlockSpec((groups, gd, gd), lambda b, i: (0, 0, 0)),
                  pl.BlockSpec((groups, 1, gd), lambda b, i: (0, 0, 0))],
        out_specs=pl.BlockSpec((ts, width), lambda b, i: (b * ns + i, 0)),
        out_shape=jax.ShapeDtypeStruct((n, width), BF16),
        scratch_shapes=[pltpu.VMEM((POOL_HALO + ts, gd), F32)],
        compiler_params=_cparams(("parallel", "arbitrary")),
        name="pool_mix",
    )(proj, proj, pool_w, pool_scale.reshape(groups, 1, gd))


def _gated_kernel(oa_ref, pm_ref, wa_ref, wp_ref, ga_ref, gp_ref, o_ref):
    ya = jnp.dot(oa_ref[...], wa_ref[...], preferred_element_type=F32)
    yp = jnp.dot(pm_ref[...], wp_ref[...], preferred_element_type=F32)
    ga = jax.nn.sigmoid(ga_ref[...].astype(F32))
    gp = jax.nn.sigmoid(gp_ref[...].astype(F32))
    o_ref[...] = (ga * ya + gp * yp).astype(o_ref.dtype)


def _gated_merge(o_attn, p_mixed, w_a, w_p, proj, gate_col0):
    n, ka = o_attn.shape
    _, kp = p_mixed.shape
    d = w_a.shape[1]
    tm, tn = _tile(1024, n), _tile(512, d)
    gb = gate_col0 // tn
    nb = d // tn
    return pl.pallas_call(
        _gated_kernel,
        grid=(n // tm, nb),
        in_specs=[pl.BlockSpec((tm, ka), lambda i, j: (i, 0)),
                  pl.BlockSpec((tm, kp), lambda i, j: (i, 0)),
                  pl.BlockSpec((ka, tn), lambda i, j: (0, j)),
                  pl.BlockSpec((kp, tn), lambda i, j: (0, j)),
                  pl.BlockSpec((tm, tn), lambda i, j: (i, gb + j)),
                  pl.BlockSpec((tm, tn), lambda i, j: (i, gb + nb + j))],
        out_specs=pl.BlockSpec((tm, tn), lambda i, j: (i, j)),
        out_shape=jax.ShapeDtypeStruct((n, d), BF16),
        compiler_params=_cparams(("parallel", "parallel")),
        name="gated_merge",
    )(o_attn, p_mixed, w_a, w_p, proj, proj)


F32_SUBLANES = 8


def _token_pitch(d):
    return d // LANES + F32_SUBLANES


def _split2(x):
    hi = x.astype(BF16)
    lo = (x - hi.astype(F32)).astype(BF16)
    return hi, lo


def _norm_route_kernel(x_ref, g_ref, wr_ref, br_ref, hp_ref, ids_ref, wts_ref, *,
                       n_groups, per_group):
    x = x_ref[...]
    tm, d = x.shape
    ms = jnp.mean(x * x, axis=-1, keepdims=True)
    h = x * lax.rsqrt(ms + RMS_EPS) * g_ref[...]
    pitch = _token_pitch(d)
    for c in range(d // LANES):
        hp_ref[pl.ds(c, tm, stride=pitch), :] = h[:, c * LANES:(c + 1) * LANES]
    for c in range(d // LANES, pitch):
        hp_ref[pl.ds(c, tm, stride=pitch), :] = jnp.zeros((tm, LANES), F32)

    h2 = _split2(h)
    w2 = _split2(wr_ref[...])
    logits = br_ref[...] + jnp.zeros((tm, LANES), F32)
    for a in range(2):
        for b in range(2 - a):
            logits = logits + jnp.dot(h2[a], w2[b], preferred_element_type=F32)

    col = lax.broadcasted_iota(I32, (tm, LANES), 1).astype(F32)
    is_grp = col < n_groups
    gl = jnp.where(is_grp, logits, NEG_BIG)
    gmax = jnp.max(gl, axis=-1, keepdims=True)
    g_top = jnp.min(jnp.where(gl == gmax, col, float(LANES)), axis=-1, keepdims=True)
    denom = jnp.sum(jnp.where(is_grp, jnp.exp(gl - gmax), 0.0), axis=-1, keepdims=True)
    p_group = 1.0 / denom
    first = n_groups + g_top * per_group
    in_grp = (col >= first) & (col < first + per_group)
    el = jnp.where(in_grp, logits, NEG_BIG)
    v1 = jnp.max(el, axis=-1, keepdims=True)
    i1 = jnp.min(jnp.where(el == v1, col, float(LANES)), axis=-1, keepdims=True)
    el2 = jnp.where(col == i1, NEG_BIG, el)
    v2 = jnp.max(el2, axis=-1, keepdims=True)
    i2 = jnp.min(jnp.where(el2 == v2, col, float(LANES)), axis=-1, keepdims=True)
    e2 = jnp.exp(v2 - v1)
    w1 = p_group / (1.0 + e2)
    w2 = p_group * e2 / (1.0 + e2)
    ids = jnp.where(col == 0.0, i1 - n_groups, jnp.where(col == 1.0, i2 - n_groups, 0.0))
    ids_ref[...] = ids.astype(I32)
    wts_ref[...] = jnp.where(col == 0.0, w1, jnp.where(col == 1.0, w2, 0.0))


def _norm_route(x, g, w_rg, b_rg, w_re, b_re):
    n, d = x.shape
    n_groups = w_rg.shape[1]
    n_experts = w_re.shape[1]
    assert n_groups + n_experts <= LANES
    pad = LANES - n_groups - n_experts
    wr = jnp.concatenate([w_rg, w_re, jnp.zeros((d, pad), F32)], axis=1)
    br = jnp.concatenate([b_rg, b_re, jnp.zeros((pad,), F32)]).reshape(1, LANES)
    tm = _tile(256, n)
    kern = functools.partial(_norm_route_kernel, n_groups=n_groups,
                             per_group=n_experts // n_groups)
    return pl.pallas_call(
        kern,
        grid=(n // tm,),
        in_specs=[pl.BlockSpec((tm, d), lambda i: (i, 0)),
                  pl.BlockSpec((1, d), lambda i: (0, 0)),
                  pl.BlockSpec((d, LANES), lambda i: (0, 0)),
                  pl.BlockSpec((1, LANES), lambda i: (0, 0))],
        out_specs=[pl.BlockSpec((tm * _token_pitch(d), LANES), lambda i: (i, 0)),
                   pl.BlockSpec((tm, LANES), lambda i: (i, 0)),
                   pl.BlockSpec((tm, LANES), lambda i: (i, 0))],
        out_shape=[jax.ShapeDtypeStruct((n * _token_pitch(d), LANES), F32),
                   jax.ShapeDtypeStruct((n, LANES), I32),
                   jax.ShapeDtypeStruct((n, LANES), F32)],
        compiler_params=_cparams(("parallel",)),
        name="norm_route",
    )(x, g.reshape(1, d), wr, br)


def _start_token_copies(idx_ref, base, src_hbm, dst, sem, first, count, chunks, pitch):
    for r in range(first, first + count):
        row = pl.multiple_of(idx_ref[base + r] * pitch, F32_SUBLANES)
        pltpu.make_async_copy(src_hbm.at[pl.ds(row, chunks)], dst.at[pl.ds(r * pitch, chunks)],
                              sem).start()


def _expert_up_kernel(tok_ref, te_ref, valid_ref, h_hbm, wg_ref, wu_ref, o_ref, xbuf, xb, sem, *,
                      tm, fc):
    t = pl.program_id(0)
    last = pl.num_programs(0) - 1
    slot = t % 2
    valid = valid_ref[t] != 0
    f = o_ref.shape[1]
    nseg = 2 * (f // fc)
    per_seg = tm // nseg
    chunks = xb.shape[1] // LANES
    pitch = xbuf.shape[1] // tm

    @pl.when(t == 0)
    def _():
        _start_token_copies(tok_ref, 0, h_hbm, xbuf.at[0], sem.at[0], 0, tm, chunks, pitch)

    pltpu.make_async_copy(h_hbm.at[pl.ds(0, tm * chunks)], xbuf.at[slot, pl.ds(0, tm * chunks)],
                          sem.at[slot]).wait()

    def prefetch(seg):
        _start_token_copies(tok_ref, (t + 1) * tm, h_hbm, xbuf.at[1 - slot], sem.at[1 - slot],
                            seg * per_seg, per_seg, chunks, pitch)

    def ffn(more):
        for c in range(chunks):
            xb[:, c * LANES:(c + 1) * LANES] = xbuf[slot, pl.ds(c, tm, stride=pitch), :].astype(BF16)
        for ci, c in enumerate(range(0, f, fc)):
            if more:
                prefetch(2 * ci)
            g = jnp.dot(xb[...], wg_ref[0, :, c:c + fc], preferred_element_type=F32)
            if more:
                prefetch(2 * ci + 1)
            u = jnp.dot(xb[...], wu_ref[0, :, c:c + fc], preferred_element_type=F32)
            o_ref[:, c:c + fc] = (g * jax.nn.sigmoid(g) * u).astype(o_ref.dtype)

    def skip(more):
        if more:
            for seg in range(nseg):
                prefetch(seg)
        o_ref[...] = jnp.zeros_like(o_ref)

    for more, live in ((True, True), (True, False), (False, True), (False, False)):
        @pl.when(((t < last) == more) & (valid == live))
        def _(more=more, live=live):
            ffn(more) if live else skip(more)


def _expert_up(h, row_tok, w_gate, w_up, tile_e, tile_valid, tm):
    rows = row_tok.shape[0]
    _, d, f = w_gate.shape
    nt = rows // tm
    fc = _tile(256, f)
    return pl.pallas_call(
        functools.partial(_expert_up_kernel, tm=tm, fc=fc),
        grid_spec=pltpu.PrefetchScalarGridSpec(
            num_scalar_prefetch=3,
            grid=(nt,),
            in_specs=[pl.BlockSpec(memory_space=pl.ANY),
                      pl.BlockSpec((1, d, f), lambda t, tok, te, tv: (te[t], 0, 0)),
                      pl.BlockSpec((1, d, f), lambda t, tok, te, tv: (te[t], 0, 0))],
            out_specs=pl.BlockSpec((tm, f), lambda t, tok, te, tv: (t, 0)),
            scratch_shapes=[pltpu.VMEM((2, tm * _token_pitch(d), LANES), F32), pltpu.VMEM((tm, d), BF16),
                            pltpu.SemaphoreType.DMA((2,))]),
        out_shape=jax.ShapeDtypeStruct((rows, f), BF16),
        compiler_params=_cparams(("arbitrary",)),
        name="expert_up",
    )(row_tok, tile_e, tile_valid, h, w_gate, w_up)


def _expert_down_kernel(te_ref, valid_ref, h_ref, wd_ref, rw_ref, o_ref):
    t = pl.program_id(0)

    @pl.when(valid_ref[t] != 0)
    def _():
        y = jnp.dot(h_ref[...], wd_ref[0], preferred_element_type=F32)
        o_ref[...] = (y * rw_ref[...]).astype(o_ref.dtype)

    @pl.when(valid_ref[t] == 0)
    def _():
        o_ref[...] = jnp.zeros_like(o_ref)


def _expert_down(hact, w_down, row_w, tile_e, tile_valid, tm):
    rows, f = hact.shape
    _, _, d = w_down.shape
    nt = rows // tm
    return pl.pallas_call(
        _expert_down_kernel,
        grid_spec=pltpu.PrefetchScalarGridSpec(
            num_scalar_prefetch=2,
            grid=(nt,),
            in_specs=[pl.BlockSpec((tm, f), lambda t, te, tv: (t, 0)),
                      pl.BlockSpec((1, f, d), lambda t, te, tv: (te[t], 0, 0)),
                      pl.BlockSpec((tm, 1), lambda t, te, tv: (t, 0))],
            out_specs=pl.BlockSpec((tm, d), lambda t, te, tv: (t, 0))),
        out_shape=jax.ShapeDtypeStruct((rows, d), F32),
        compiler_params=_cparams(("arbitrary",)),
        name="expert_down",
    )(tile_e, tile_valid, hact, w_down, row_w.reshape(rows, 1))


COMBINE_ROWS = 8


def _combine_kernel(pos_ref, x_ref, ys_hbm, g_ref, o_ref, buf, sem, *, tm, final_norm):
    i = pl.program_id(0)
    last = pl.num_programs(0) - 1
    slot = i % 2

    def gather(tile, s):
        for k in range(TOP_K):
            for r in range(tm):
                p = pos_ref[TOP_K * (tile * tm + r) + k]
                pltpu.make_async_copy(ys_hbm.at[pl.ds(p, 1)], buf.at[s, k, pl.ds(r, 1)],
                                      sem.at[s, k]).start()

    @pl.when(i == 0)
    def _():
        gather(0, 0)

    for k in range(TOP_K):
        pltpu.make_async_copy(ys_hbm.at[pl.ds(0, tm)], buf.at[slot, k], sem.at[slot, k]).wait()

    def finish():
        for c in range(0, tm, COMBINE_ROWS):
            rows = slice(c, c + COMBINE_ROWS)
            y = x_ref[rows, :]
            for k in range(TOP_K):
                y = y + buf[slot, k, rows, :]
            if final_norm:
                ms = jnp.mean(y * y, axis=-1, keepdims=True)
                y = y * lax.rsqrt(ms + RMS_EPS) * g_ref[...]
            o_ref[rows, :] = y

    @pl.when(i < last)
    def _():
        gather(i + 1, 1 - slot)
        finish()

    @pl.when(i == last)
    def _():
        finish()


def _combine(x1, ys, pos, g, final_norm):
    n, d = x1.shape
    tm = _tile(128, n)
    return pl.pallas_call(
        functools.partial(_combine_kernel, tm=tm, final_norm=final_norm),
        grid_spec=pltpu.PrefetchScalarGridSpec(
            num_scalar_prefetch=1,
            grid=(n // tm,),
            in_specs=[pl.BlockSpec((tm, d), lambda i, pos: (i, 0)),
                      pl.BlockSpec(memory_space=pl.ANY),
                      pl.BlockSpec((1, d), lambda i, pos: (0, 0))],
            out_specs=pl.BlockSpec((tm, d), lambda i, pos: (i, 0)),
            scratch_shapes=[pltpu.VMEM((2, TOP_K, tm, d), F32),
                            pltpu.SemaphoreType.DMA((2, TOP_K))]),
        out_shape=jax.ShapeDtypeStruct((n, d), F32),
        compiler_params=_cparams(("arbitrary",)),
        name="moe_combine",
    )(pos, x1, ys, g.reshape(1, d))


def _dispatch_plan(expert_ids, weights, n_experts, tm):
    n = expert_ids.shape[0]
    a = n * TOP_K
    flat_e = expert_ids.reshape(a)
    onehot = (flat_e[:, None] == jnp.arange(n_experts, dtype=I32)[None, :]).astype(I32)
    incl = jnp.cumsum(onehot, axis=0)
    rank = jnp.sum((incl - onehot) * onehot, axis=1)
    counts = incl[-1]
    padded = (counts + tm - 1) // tm * tm
    pends = jnp.cumsum(padded)
    pstarts = pends - padded
    dest = (pstarts[flat_e] + rank).astype(I32)
    n_tiles = a // tm + n_experts
    rows = n_tiles * tm
    fields = jnp.stack([(jnp.arange(a, dtype=I32) // TOP_K).astype(F32), weights.reshape(a)], axis=1)
    row_fields = jnp.zeros((rows, 2), F32).at[dest].set(fields)
    row_tok = row_fields[:, 0].astype(I32)
    row_w = row_fields[:, 1]
    tile_start = jnp.arange(n_tiles, dtype=I32) * tm
    tile_e = jnp.minimum(jnp.sum((tile_start[:, None] >= pends[None, :]).astype(I32), axis=1),
                         n_experts - 1)
    tile_valid = (tile_start < pends[-1]).astype(I32)
    return row_tok, row_w, dest, tile_e, tile_valid


def kernel(x, norm_mix, w_in, w_o_attn, pool_w, pool_scale, w_o_pool, w_out, norm_ffn,
           w_router_group, b_router_group, w_router_expert, b_router_expert,
           w_gate, w_up, w_down, norm_final):
    batch, seq, d = x.shape
    n = batch * seq
    depth = norm_mix.shape[0]
    sb_width = w_o_attn.shape[1]
    n_heads = sb_width // HEAD_DIM
    groups, gd = pool_w.shape[1], pool_w.shape[2]
    pool_width = groups * gd
    n_experts = w_router_expert.shape[2]
    moe_tm = 256
    assert pool_width == sb_width and (3 * sb_width) % pool_width == 0

    xf = x.reshape(n, d)
    for l in range(depth):
        h = _rmsnorm(xf, norm_mix[l], BF16)
        col_scale = jnp.where(jnp.arange(w_in.shape[2]) < sb_width,
                              math.log2(math.e) / math.sqrt(HEAD_DIM), 1.0)
        w_in_l = (w_in[l] * col_scale[None, :].astype(F32)).astype(BF16)
        proj = _matmul(h, w_in_l, BF16, tm=1024, tn=1024, name="in_proj")
        d_expert = w_gate.shape[3]
        o_attn, (wg_b, wu_b, wd_b) = _attention(
            proj, batch, seq, n_heads,
            to_bf16=(w_gate[l].reshape(n_experts * d, d_expert), w_up[l].reshape(n_experts * d, d_expert),
                     w_down[l].reshape(n_experts * d_expert, d)))
        wg_b = wg_b.reshape(n_experts, d, d_expert)
        wu_b = wu_b.reshape(n_experts, d, d_expert)
        wd_b = wd_b.reshape(n_experts, d_expert, d)
        p_mixed = _pool_mix(proj, pool_w[l].astype(BF16), pool_scale[l], batch, seq,
                            (3 * sb_width) // pool_width)
        merged = _gated_merge(o_attn, p_mixed, w_o_attn[l].astype(BF16), w_o_pool[l].astype(BF16),
                              proj, 3 * sb_width + pool_width)
        x1 = _matmul(merged, w_out[l].astype(BF16), F32, tm=512, tn=1024, residual=xf,
                     name="out_proj")
        hp, ids_slab, wts_slab = _norm_route(x1, norm_ffn[l], w_router_group[l], b_router_group[l],
                                             w_router_expert[l], b_router_expert[l])
        row_tok, row_w, pos, tile_e, tile_valid = _dispatch_plan(
            ids_slab[:, :TOP_K], wts_slab[:, :TOP_K], n_experts, moe_tm)
        hact = _expert_up(hp, row_tok, wg_b, wu_b, tile_e, tile_valid, moe_tm)
        ys = _expert_down(hact, wd_b, row_w, tile_e, tile_valid, moe_tm)
        last = l == depth - 1
        xf = _combine(x1, ys, pos, norm_final, final_norm=last)
    return xf.reshape(batch, seq, d)
```

```python
import functools
import math

import jax
import jax.numpy as jnp
from jax import lax
from jax.experimental import pallas as pl
from jax.experimental.pallas import tpu as pltpu

F32 = jnp.float32
BF16 = jnp.bfloat16
U32 = jnp.uint32
I32 = jnp.int32

HEAD_DIM = 128
POOL_SIZES = (2, 4, 8, 16)
POOL_HALO = 16
TOP_K = 2
RMS_EPS = 1e-6
LANES = 128
VMEM_LIMIT = 56 * 1024 * 1024
NEG_BIG = -1e30


def _cparams(sem):
    return pltpu.CompilerParams(dimension_semantics=sem, vmem_limit_bytes=VMEM_LIMIT)


def _tile(pref, dim):
    t = min(pref, dim)
    assert dim % t == 0, (pref, dim)
    return t


def _rmsnorm_kernel(x_ref, g_ref, o_ref):
    x = x_ref[...]
    ms = jnp.mean(x * x, axis=-1, keepdims=True)
    o_ref[...] = (x * lax.rsqrt(ms + RMS_EPS) * g_ref[...]).astype(o_ref.dtype)


def _rmsnorm(x, g, out_dtype):
    n, d = x.shape
    tm = _tile(256, n)
    return pl.pallas_call(
        _rmsnorm_kernel,
        grid=(n // tm,),
        in_specs=[pl.BlockSpec((tm, d), lambda i: (i, 0)),
                  pl.BlockSpec((1, d), lambda i: (0, 0))],
        out_specs=pl.BlockSpec((tm, d), lambda i: (i, 0)),
        out_shape=jax.ShapeDtypeStruct((n, d), out_dtype),
        compiler_params=_cparams(("parallel",)),
        name="rmsnorm",
    )(x, g.reshape(1, d))


def _mm_kernel(a_ref, b_ref, o_ref):
    o_ref[...] = jnp.dot(a_ref[...], b_ref[...], preferred_element_type=F32).astype(o_ref.dtype)


def _mm_res_kernel(a_ref, b_ref, r_ref, o_ref):
    o_ref[...] = (r_ref[...] + jnp.dot(a_ref[...], b_ref[...], preferred_element_type=F32)
                  ).astype(o_ref.dtype)


def _matmul(a, b, out_dtype, *, tm, tn, residual=None, name):
    m, k = a.shape
    _, n = b.shape
    tm, tn = _tile(tm, m), _tile(tn, n)
    in_specs = [pl.BlockSpec((tm, k), lambda i, j: (i, 0)),
                pl.BlockSpec((k, tn), lambda i, j: (0, j))]
    args = [a, b]
    body = _mm_kernel
    if residual is not None:
        in_specs.append(pl.BlockSpec((tm, tn), lambda i, j: (i, j)))
        args.append(residual)
        body = _mm_res_kernel
    return pl.pallas_call(
        body,
        grid=(m // tm, n // tn),
        in_specs=in_specs,
        out_specs=pl.BlockSpec((tm, tn), lambda i, j: (i, j)),
        out_shape=jax.ShapeDtypeStruct((m, n), out_dtype),
        compiler_params=_cparams(("parallel", "parallel")),
        name=name,
    )(*args)


ATTN_SUB = 256
ATTN_HEADS = 2


def _cast_stream_step(step, total_steps, srcs, dsts, inbufs, outbufs, in_sem, out_sem):
    slot = step % 2

    def rows(j):
        return inbufs[j].shape[1]

    def read(j, chunk, sl):
        start = pl.multiple_of(chunk * rows(j), rows(j))
        return pltpu.make_async_copy(srcs[j].at[pl.ds(start, rows(j))], inbufs[j].at[sl],
                                     in_sem.at[j, sl])

    def write(j, chunk, sl):
        start = pl.multiple_of(chunk * rows(j), rows(j))
        return pltpu.make_async_copy(outbufs[j].at[sl], dsts[j].at[pl.ds(start, rows(j))],
                                     out_sem.at[j, sl])

    by_chunks = {}
    for j in range(len(srcs)):
        by_chunks.setdefault(srcs[j].shape[0] // rows(j), []).append(j)

    for nchunks, js in by_chunks.items():
        assert 1 <= nchunks <= total_steps

        @pl.when(step == 0)
        def _(js=js):
            for j in js:
                read(j, 0, 0).start()

        @pl.when(step + 1 < nchunks)
        def _(js=js):
            for j in js:
                read(j, step + 1, 1 - slot).start()

        @pl.when((step >= 2) & (step < nchunks))
        def _(js=js):
            for j in js:
                write(j, step - 2, slot).wait()

        @pl.when(step < nchunks)
        def _(js=js):
            for j in js:
                read(j, step, slot).wait()
                outbufs[j][slot] = inbufs[j][slot].astype(outbufs[j].dtype)
                write(j, step, slot).start()

        @pl.when(step == nchunks - 1)
        def _(js=js, nchunks=nchunks):
            for j in js:
                write(j, step, slot).wait()
                if nchunks >= 2:
                    write(j, step - 1, 1 - slot).wait()


def _attn_kernel(q_ref, k_ref, v_ref, *rest, tq, sub, heads, n_stream, grid):
    ns = n_stream
    srcs, o_ref, dsts = rest[:ns], rest[ns], rest[ns + 1:2 * ns + 1]
    acc_ref, carry_ref = rest[2 * ns + 1:2 * ns + 3]
    inbufs, outbufs = rest[2 * ns + 3:3 * ns + 3], rest[3 * ns + 3:4 * ns + 3]
    _attn_tile(q_ref, k_ref, v_ref, o_ref, acc_ref, carry_ref, tq=tq, sub=sub, heads=heads)
    if ns:
        in_sem, out_sem = rest[4 * ns + 3:]
        step = (pl.program_id(0) * grid[1] + pl.program_id(1)) * grid[2] + pl.program_id(2)
        _cast_stream_step(step, grid[0] * grid[1] * grid[2], srcs, dsts, inbufs, outbufs,
                          in_sem, out_sem)


def _attn_tile(q_ref, k_ref, v_ref, o_ref, acc_ref, carry_ref, *, tq, sub, heads):
    i = pl.program_id(2)
    nsub = tq // sub
    qi = lax.broadcasted_iota(I32, (sub, sub), 0)
    ki = lax.broadcasted_iota(I32, (sub, sub), 1)
    tri = jnp.where(qi >= ki, 1.0, 0.0).astype(BF16)
    causal = ki < qi

    def attend(rows, start, nkeys, masked, carries, accs):
        nrows = rows.stop - rows.start
        start = pl.multiple_of(start, sub)
        hcols = [slice(h * HEAD_DIM, (h + 1) * HEAD_DIM) for h in range(heads)]
        z = [lax.dot_general(q_ref[rows, hc], k_ref[pl.ds(start, nkeys), hc],
                             (((1,), (1,)), ((), ())), preferred_element_type=F32) for hc in hcols]
        sp = [jnp.maximum(zh, 0.0) + jnp.log2(1.0 + jnp.exp2(-jnp.abs(zh))) for zh in z]
        if masked:
            sp = [jnp.where(causal, s, 0.0) for s in sp]
        chunks = [slice(c, c + sub) for c in range(0, nkeys, sub)]
        spb = [s.astype(BF16) for s in sp]
        cs = [[jnp.dot(s[:, ck], tri, preferred_element_type=F32) for ck in chunks] for s in spb]
        tot = [[jnp.broadcast_to(c[:, 0:1], (nrows, HEAD_DIM)) for c in ch] for ch in cs]
        new_carries, new_accs = [], []
        for h in range(heads):
            carry = carries[h]
            w = [None] * (nkeys // LANES)
            for cb in range(len(chunks) - 1, -1, -1):
                for c0 in range(0, sub, LANES):
                    g0 = cb * sub + c0
                    w[g0 // LANES] = jnp.exp2(z[h][:, g0:g0 + LANES] - cs[h][cb][:, c0:c0 + LANES]
                                              - carry)
                carry = carry + tot[h][cb]
            w = jnp.concatenate(w, axis=1)
            if masked:
                w = jnp.where(causal, w, 0.0)
            new_carries.append(carry)
            new_accs.append(accs[h] + jnp.dot(w.astype(BF16), v_ref[pl.ds(start, nkeys), hcols[h]],
                                              preferred_element_type=F32))
        return new_carries, new_accs

    base = i * tq
    for rb in range(nsub):
        rows = slice(rb * sub, (rb + 1) * sub)
        zero = [jnp.zeros((sub, HEAD_DIM), F32)] * heads
        carries, accs = attend(rows, base + rb * sub, sub, True, zero, zero)
        for cb in range(rb - 1, -1, -1):
            carries, accs = attend(rows, base + cb * sub, sub, False, carries, accs)
        acc_ref[rows, :] = jnp.concatenate(accs, axis=1)
        carry_ref[rows, :] = jnp.concatenate(carries, axis=1)

    def body(n, c):
        full = slice(0, tq)
        hcols = [slice(h * HEAD_DIM, (h + 1) * HEAD_DIM) for h in range(heads)]
        carries, accs = attend(full, (i - 1 - n) * tq, tq, False,
                               [carry_ref[:, hc] for hc in hcols], [acc_ref[:, hc] for hc in hcols])
        acc_ref[...] = jnp.concatenate(accs, axis=1)
        carry_ref[...] = jnp.concatenate(carries, axis=1)
        return c

    lax.fori_loop(0, i, body, 0)
    o_ref[...] = acc_ref[...].astype(o_ref.dtype)


BF16_SUBLANES = 16


def _attention(proj, batch, seq, n_heads, to_bf16=()):
    n = batch * seq
    tq = _tile(512, seq)
    sub = _tile(ATTN_SUB, tq)
    nq = seq // tq
    heads = _tile(ATTN_HEADS, n_heads)
    ng = n_heads // heads
    width = heads * HEAD_DIM
    grid = (batch, ng, nq)
    nsteps = batch * ng * nq
    ns = len(to_bf16)
    chunk_rows = [max(BF16_SUBLANES, a.shape[0] // nsteps) for a in to_bf16]
    for a, r in zip(to_bf16, chunk_rows):
        assert a.shape[0] % r == 0 and r % BF16_SUBLANES == 0 and a.shape[0] // r <= nsteps, (
            a.shape, nsteps)
    any_spec = pl.BlockSpec(memory_space=pl.ANY)
    kern = functools.partial(_attn_kernel, tq=tq, sub=sub, heads=heads, n_stream=ns, grid=grid)
    outs = pl.pallas_call(
        kern,
        grid=grid,
        in_specs=[pl.BlockSpec((tq, width), lambda b, g, i: (b * nq + i, g)),
                  pl.BlockSpec((seq, width), lambda b, g, i: (b, ng + g)),
                  pl.BlockSpec((seq, width), lambda b, g, i: (b, 2 * ng + g))] + [any_spec] * ns,
        out_specs=[pl.BlockSpec((tq, width), lambda b, g, i: (b * nq + i, g))] + [any_spec] * ns,
        out_shape=[jax.ShapeDtypeStruct((n, n_heads * HEAD_DIM), BF16)]
        + [jax.ShapeDtypeStruct(a.shape, BF16) for a in to_bf16],
        scratch_shapes=[pltpu.VMEM((tq, width), F32), pltpu.VMEM((tq, width), F32)]
        + [pltpu.VMEM((2, r, a.shape[1]), F32) for a, r in zip(to_bf16, chunk_rows)]
        + [pltpu.VMEM((2, r, a.shape[1]), BF16) for a, r in zip(to_bf16, chunk_rows)]
        + ([pltpu.SemaphoreType.DMA((ns, 2)), pltpu.SemaphoreType.DMA((ns, 2))] if ns else []),
        compiler_params=_cparams(("arbitrary", "arbitrary", "arbitrary")),
        name="stickbreak_attn",
    )(proj, proj, proj, *to_bf16)
    return outs[0], tuple(outs[1:])


def _pool_kernel(u_ref, halo_ref, w_ref, s_ref, o_ref, buf, *, ts, gd):
    i = pl.program_id(1)
    pos = i * ts + lax.broadcasted_iota(I32, (ts, gd), 0)
    for g, win in enumerate(POOL_SIZES):
        cols = slice(g * gd, (g + 1) * gd)
        u = u_ref[:, cols].astype(F32)
        halo = halo_ref[:, cols].astype(F32)
        buf[0:POOL_HALO, :] = jnp.where(i > 0, halo, 0.0)
        buf[POOL_HALO:POOL_HALO + ts, :] = u
        acc = u
        for k in range(1, win):
            acc = acc + buf[POOL_HALO - k:POOL_HALO - k + ts, :]
        cnt = jnp.minimum(pos + 1, win).astype(F32)
        pooled = acc / cnt - u
        mixed = jnp.dot(pooled.astype(BF16), w_ref[g], preferred_element_type=F32) * s_ref[g]
        o_ref[:, cols] = mixed.astype(o_ref.dtype)


def _pool_mix(proj, pool_w, pool_scale, batch, seq, col_block):
    n = batch * seq
    groups, gd, _ = pool_w.shape
    width = groups * gd
    ts = _tile(512, seq)
    ns = seq // ts
    hb = ts // POOL_HALO
    kern = functools.partial(_pool_kernel, ts=ts, gd=gd)
    return pl.pallas_call(
        kern,
        grid=(batch, ns),
        in_specs=[pl.BlockSpec((ts, width), lambda b, i: (b * ns + i, col_block)),
                  pl.BlockSpec((POOL_HALO, width),
                               lambda b, i: (jnp.maximum((b * ns + i) * hb - 1, 0), col_block)),
                  pl.BlockSpec((groups, gd, gd), lambda b, i: (0, 0, 0)),
                  pl.BlockSpec((groups, 1, gd), lambda b, i: (0, 0, 0))],
        out_specs=pl.BlockSpec((ts, width), lambda b, i: (b * ns + i, 0)),
        out_shape=jax.ShapeDtypeStruct((n, width), BF16),
        scratch_shapes=[pltpu.VMEM((POOL_HALO + ts, gd), F32)],
        compiler_params=_cparams(("parallel", "arbitrary")),
        name="pool_mix",
    )(proj, proj, pool_w, pool_scale.reshape(groups, 1, gd))


def _gated_kernel(oa_ref, pm_ref, wa_ref, wp_ref, ga_ref, gp_ref, o_ref):
    ya = jnp.dot(oa_ref[...], wa_ref[...], preferred_element_type=F32)
    yp = jnp.dot(pm_ref[...], wp_ref[...], preferred_element_type=F32)
    ga = jax.nn.sigmoid(ga_ref[...].astype(F32))
    gp = jax.nn.sigmoid(gp_ref[...].astype(F32))
    o_ref[...] = (ga * ya + gp * yp).astype(o_ref.dtype)


def _gated_merge(o_attn, p_mixed, w_a, w_p, proj, gate_col0):
    n, ka = o_attn.shape
    _, kp = p_mixed.shape
    d = w_a.shape[1]
    tm, tn = _tile(1024, n), _tile(512, d)
    gb = gate_col0 // tn
    nb = d // tn
    return pl.pallas_call(
        _gated_kernel,
        grid=(n // tm, nb),
        in_specs=[pl.BlockSpec((tm, ka), lambda i, j: (i, 0)),
                  pl.BlockSpec((tm, kp), lambda i, j: (i, 0)),
                  pl.BlockSpec((ka, tn), lambda i, j: (0, j)),
                  pl.BlockSpec((kp, tn), lambda i, j: (0, j)),
                  pl.BlockSpec((tm, tn), lambda i, j: (i, gb + j)),
                  pl.BlockSpec((tm, tn), lambda i, j: (i, gb + nb + j))],
        out_specs=pl.BlockSpec((tm, tn), lambda i, j: (i, j)),
        out_shape=jax.ShapeDtypeStruct((n, d), BF16),
        compiler_params=_cparams(("parallel", "parallel")),
        name="gated_merge",
    )(o_attn, p_mixed, w_a, w_p, proj, proj)


TOKEN_PAD_ROWS = 4


def _token_pitch(d):
    return d // LANES + TOKEN_PAD_ROWS


def _split2(x):
    hi = x.astype(BF16)
    lo = (x - hi.astype(F32)).astype(BF16)
    return hi, lo


def _norm_route_kernel(x_ref, g_ref, wr_ref, br_ref, hp_ref, ids_ref, wts_ref, *,
                       n_groups, per_group):
    x = x_ref[...]
    tm, d = x.shape
    ms = jnp.mean(x * x, axis=-1, keepdims=True)
    h = x * lax.rsqrt(ms + RMS_EPS) * g_ref[...]
    pitch = _token_pitch(d)
    for c in range(d // LANES):
        hp_ref[pl.ds(c, tm, stride=pitch), :] = h[:, c * LANES:(c + 1) * LANES]
    for c in range(d // LANES, pitch):
        hp_ref[pl.ds(c, tm, stride=pitch), :] = jnp.zeros((tm, LANES), F32)

    h2 = _split2(h)
    w2 = _split2(wr_ref[...])
    logits = br_ref[...] + jnp.zeros((tm, LANES), F32)
    for a in range(2):
        for b in range(2 - a):
            logits = logits + jnp.dot(h2[a], w2[b], preferred_element_type=F32)

    col = lax.broadcasted_iota(I32, (tm, LANES), 1).astype(F32)
    is_grp = col < n_groups
    gl = jnp.where(is_grp, logits, NEG_BIG)
    gmax = jnp.max(gl, axis=-1, keepdims=True)
    g_top = jnp.min(jnp.where(gl == gmax, col, float(LANES)), axis=-1, keepdims=True)
    denom = jnp.sum(jnp.where(is_grp, jnp.exp(gl - gmax), 0.0), axis=-1, keepdims=True)
    p_group = 1.0 / denom
    first = n_groups + g_top * per_group
    in_grp = (col >= first) & (col < first + per_group)
    el = jnp.where(in_grp, logits, NEG_BIG)
    v1 = jnp.max(el, axis=-1, keepdims=True)
    i1 = jnp.min(jnp.where(el == v1, col, float(LANES)), axis=-1, keepdims=True)
    el2 = jnp.where(col == i1, NEG_BIG, el)
    v2 = jnp.max(el2, axis=-1, keepdims=True)
    i2 = jnp.min(jnp.where(el2 == v2, col, float(LANES)), axis=-1, keepdims=True)
    e2 = jnp.exp(v2 - v1)
    w1 = p_group / (1.0 + e2)
    w2 = p_group * e2 / (1.0 + e2)
    ids = jnp.where(col == 0.0, i1 - n_groups, jnp.where(col == 1.0, i2 - n_groups, 0.0))
    ids_ref[...] = ids.astype(I32)
    wts_ref[...] = jnp.where(col == 0.0, w1, jnp.where(col == 1.0, w2, 0.0))


def _norm_route(x, g, w_rg, b_rg, w_re, b_re):
    n, d = x.shape
    n_groups = w_rg.shape[1]
    n_experts = w_re.shape[1]
    assert n_groups + n_experts <= LANES
    pad = LANES - n_groups - n_experts
    wr = jnp.concatenate([w_rg, w_re, jnp.zeros((d, pad), F32)], axis=1)
    br = jnp.concatenate([b_rg, b_re, jnp.zeros((pad,), F32)]).reshape(1, LANES)
    tm = _tile(256, n)
    kern = functools.partial(_norm_route_kernel, n_groups=n_groups,
                             per_group=n_experts // n_groups)
    return pl.pallas_call(
        kern,
        grid=(n // tm,),
        in_specs=[pl.BlockSpec((tm, d), lambda i: (i, 0)),
                  pl.BlockSpec((1, d), lambda i: (0, 0)),
                  pl.BlockSpec((d, LANES), lambda i: (0, 0)),
                  pl.BlockSpec((1, LANES), lambda i: (0, 0))],
        out_specs=[pl.BlockSpec((tm * _token_pitch(d), LANES), lambda i: (i, 0)),
                   pl.BlockSpec((tm, LANES), lambda i: (i, 0)),
                   pl.BlockSpec((tm, LANES), lambda i: (i, 0))],
        out_shape=[jax.ShapeDtypeStruct((n * _token_pitch(d), LANES), F32),
                   jax.ShapeDtypeStruct((n, LANES), I32),
                   jax.ShapeDtypeStruct((n, LANES), F32)],
        compiler_params=_cparams(("parallel",)),
        name="norm_route",
    )(x, g.reshape(1, d), wr, br)


def _start_token_copies(idx_ref, base, src_hbm, dst, sem, first, count, chunks, pitch):
    for r in range(first, first + count):
        row = idx_ref[base + r] * pitch
        pltpu.make_async_copy(src_hbm.at[pl.ds(row, chunks)], dst.at[pl.ds(r * pitch, chunks)],
                              sem).start()


def _expert_up_kernel(tok_ref, te_ref, valid_ref, h_hbm, wg_ref, wu_ref, o_ref, xbuf, xb, sem, *,
                      tm, fc):
    t = pl.program_id(0)
    last = pl.num_programs(0) - 1
    slot = t % 2
    valid = valid_ref[t] != 0
    f = o_ref.shape[1]
    nseg = 2 * (f // fc)
    per_seg = tm // nseg
    chunks = xb.shape[1] // LANES
    pitch = xbuf.shape[1] // tm

    @pl.when(t == 0)
    def _():
        _start_token_copies(tok_ref, 0, h_hbm, xbuf.at[0], sem.at[0], 0, tm, chunks, pitch)

    pltpu.make_async_copy(h_hbm.at[pl.ds(0, tm * chunks)], xbuf.at[slot, pl.ds(0, tm * chunks)],
                          sem.at[slot]).wait()

    def prefetch(seg):
        _start_token_copies(tok_ref, (t + 1) * tm, h_hbm, xbuf.at[1 - slot], sem.at[1 - slot],
                            seg * per_seg, per_seg, chunks, pitch)

    def ffn(more):
        for c in range(chunks):
            xb[:, c * LANES:(c + 1) * LANES] = xbuf[slot, pl.ds(c, tm, stride=pitch), :].astype(BF16)
        for ci, c in enumerate(range(0, f, fc)):
            if more:
                prefetch(2 * ci)
            g = jnp.dot(xb[...], wg_ref[0, :, c:c + fc], preferred_element_type=F32)
            if more:
                prefetch(2 * ci + 1)
            u = jnp.dot(xb[...], wu_ref[0, :, c:c + fc], preferred_element_type=F32)
            o_ref[:, c:c + fc] = (g * jax.nn.sigmoid(g) * u).astype(o_ref.dtype)

    def skip(more):
        if more:
            for seg in range(nseg):
                prefetch(seg)
        o_ref[...] = jnp.zeros_like(o_ref)

    for more, live in ((True, True), (True, False), (False, True), (False, False)):
        @pl.when(((t < last) == more) & (valid == live))
        def _(more=more, live=live):
            ffn(more) if live else skip(more)


def _expert_up(h, row_tok, w_gate, w_up, tile_e, tile_valid, tm):
    rows = row_tok.shape[0]
    _, d, f = w_gate.shape
    nt = rows // tm
    fc = _tile(256, f)
    return pl.pallas_call(
        functools.partial(_expert_up_kernel, tm=tm, fc=fc),
        grid_spec=pltpu.PrefetchScalarGridSpec(
            num_scalar_prefetch=3,
            grid=(nt,),
            in_specs=[pl.BlockSpec(memory_space=pl.ANY),
                      pl.BlockSpec((1, d, f), lambda t, tok, te, tv: (te[t], 0, 0)),
                      pl.BlockSpec((1, d, f), lambda t, tok, te, tv: (te[t], 0, 0))],
            out_specs=pl.BlockSpec((tm, f), lambda t, tok, te, tv: (t, 0)),
            scratch_shapes=[pltpu.VMEM((2, tm * _token_pitch(d), LANES), F32), pltpu.VMEM((tm, d), BF16),
                            pltpu.SemaphoreType.DMA((2,))]),
        out_shape=jax.ShapeDtypeStruct((rows, f), BF16),
        compiler_params=_cparams(("arbitrary",)),
        name="expert_up",
    )(row_tok, tile_e, tile_valid, h, w_gate, w_up)


def _expert_down_kernel(te_ref, valid_ref, h_ref, wd_ref, rw_ref, o_ref):
    t = pl.program_id(0)

    @pl.when(valid_ref[t] != 0)
    def _():
        y = jnp.dot(h_ref[...], wd_ref[0], preferred_element_type=F32)
        o_ref[...] = (y * rw_ref[...]).astype(o_ref.dtype)

    @pl.when(valid_ref[t] == 0)
    def _():
        o_ref[...] = jnp.zeros_like(o_ref)


def _expert_down(hact, w_down, row_w, tile_e, tile_valid, tm):
    rows, f = hact.shape
    _, _, d = w_down.shape
    nt = rows // tm
    return pl.pallas_call(
        _expert_down_kernel,
        grid_spec=pltpu.PrefetchScalarGridSpec(
            num_scalar_prefetch=2,
            grid=(nt,),
            in_specs=[pl.BlockSpec((tm, f), lambda t, te, tv: (t, 0)),
                      pl.BlockSpec((1, f, d), lambda t, te, tv: (te[t], 0, 0)),
                      pl.BlockSpec((tm, 1), lambda t, te, tv: (t, 0))],
            out_specs=pl.BlockSpec((tm, d), lambda t, te, tv: (t, 0))),
        out_shape=jax.ShapeDtypeStruct((rows, d), F32),
        compiler_params=_cparams(("arbitrary",)),
        name="expert_down",
    )(tile_e, tile_valid, hact, w_down, row_w.reshape(rows, 1))


COMBINE_ROWS = 8


def _combine_kernel(pos_ref, x_ref, ys_hbm, g_ref, o_ref, buf, sem, *, tm, final_norm):
    i = pl.program_id(0)
    last = pl.num_programs(0) - 1
    slot = i % 2

    def gather(tile, s):
        for k in range(TOP_K):
            for r in range(tm):
                p = pos_ref[TOP_K * (tile * tm + r) + k]
                pltpu.make_async_copy(ys_hbm.at[pl.ds(p, 1)], buf.at[s, k, pl.ds(r, 1)],
                                      sem.at[s, k]).start()

    @pl.when(i == 0)
    def _():
        gather(0, 0)

    for k in range(TOP_K):
        pltpu.make_async_copy(ys_hbm.at[pl.ds(0, tm)], buf.at[slot, k], sem.at[slot, k]).wait()

    def finish():
        for c in range(0, tm, COMBINE_ROWS):
            rows = slice(c, c + COMBINE_ROWS)
            y = x_ref[rows, :]
            for k in range(TOP_K):
                y = y + buf[slot, k, rows, :]
            if final_norm:
                ms = jnp.mean(y * y, axis=-1, keepdims=True)
                y = y * lax.rsqrt(ms + RMS_EPS) * g_ref[...]
            o_ref[rows, :] = y

    @pl.when(i < last)
    def _():
        gather(i + 1, 1 - slot)
        finish()

    @pl.when(i == last)
    def _():
        finish()


def _combine(x1, ys, pos, g, final_norm):
    n, d = x1.shape
    tm = _tile(128, n)
    return pl.pallas_call(
        functools.partial(_combine_kernel, tm=tm, final_norm=final_norm),
        grid_spec=pltpu.PrefetchScalarGridSpec(
            num_scalar_prefetch=1,
            grid=(n // tm,),
            in_specs=[pl.BlockSpec((tm, d), lambda i, pos: (i, 0)),
                      pl.BlockSpec(memory_space=pl.ANY),
                      pl.BlockSpec((1, d), lambda i, pos: (0, 0))],
            out_specs=pl.BlockSpec((tm, d), lambda i, pos: (i, 0)),
            scratch_shapes=[pltpu.VMEM((2, TOP_K, tm, d), F32),
                            pltpu.SemaphoreType.DMA((2, TOP_K))]),
        out_shape=jax.ShapeDtypeStruct((n, d), F32),
        compiler_params=_cparams(("arbitrary",)),
        name="moe_combine",
    )(pos, x1, ys, g.reshape(1, d))


def _dispatch_plan(expert_ids, weights, n_experts, tm):
    n = expert_ids.shape[0]
    a = n * TOP_K
    flat_e = expert_ids.reshape(a)
    onehot = (flat_e[:, None] == jnp.arange(n_experts, dtype=I32)[None, :]).astype(I32)
    incl = jnp.cumsum(onehot, axis=0)
    rank = jnp.sum((incl - onehot) * onehot, axis=1)
    counts = incl[-1]
    padded = (counts + tm - 1) // tm * tm
    pends = jnp.cumsum(padded)
    pstarts = pends - padded
    dest = (pstarts[flat_e] + rank).astype(I32)
    n_tiles = a // tm + n_experts
    rows = n_tiles * tm
    fields = jnp.stack([(jnp.arange(a, dtype=I32) // TOP_K).astype(F32), weights.reshape(a)], axis=1)
    row_fields = jnp.zeros((rows, 2), F32).at[dest].set(fields)
    row_tok = row_fields[:, 0].astype(I32)
    row_w = row_fields[:, 1]
    tile_start = jnp.arange(n_tiles, dtype=I32) * tm
    tile_e = jnp.minimum(jnp.sum((tile_start[:, None] >= pends[None, :]).astype(I32), axis=1),
                         n_experts - 1)
    tile_valid = (tile_start < pends[-1]).astype(I32)
    return row_tok, row_w, dest, tile_e, tile_valid


def kernel(x, norm_mix, w_in, w_o_attn, pool_w, pool_scale, w_o_pool, w_out, norm_ffn,
           w_router_group, b_router_group, w_router_expert, b_router_expert,
           w_gate, w_up, w_down, norm_final):
    batch, seq, d = x.shape
    n = batch * seq
    depth = norm_mix.shape[0]
    sb_width = w_o_attn.shape[1]
    n_heads = sb_width // HEAD_DIM
    groups, gd = pool_w.shape[1], pool_w.shape[2]
    pool_width = groups * gd
    n_experts = w_router_expert.shape[2]
    moe_tm = 256
    assert pool_width == sb_width and (3 * sb_width) % pool_width == 0

    xf = x.reshape(n, d)
    for l in range(depth):
        h = _rmsnorm(xf, norm_mix[l], BF16)
        col_scale = jnp.where(jnp.arange(w_in.shape[2]) < sb_width,
                              math.log2(math.e) / math.sqrt(HEAD_DIM), 1.0)
        w_in_l = (w_in[l] * col_scale[None, :].astype(F32)).astype(BF16)
        proj = _matmul(h, w_in_l, BF16, tm=1024, tn=1024, name="in_proj")
        d_expert = w_gate.shape[3]
        o_attn, (wg_b, wu_b, wd_b, wa_b, wp_b, wo_b) = _attention(
            proj, batch, seq, n_heads,
            to_bf16=(w_gate[l].reshape(n_experts * d, d_expert), w_up[l].reshape(n_experts * d, d_expert),
                     w_down[l].reshape(n_experts * d_expert, d), w_o_attn[l], w_o_pool[l], w_out[l]))
        wg_b = wg_b.reshape(n_experts, d, d_expert)
        wu_b = wu_b.reshape(n_experts, d, d_expert)
        wd_b = wd_b.reshape(n_experts, d_expert, d)
        p_mixed = _pool_mix(proj, pool_w[l].astype(BF16), pool_scale[l], batch, seq,
                            (3 * sb_width) // pool_width)
        merged = _gated_merge(o_attn, p_mixed, wa_b, wp_b, proj, 3 * sb_width + pool_width)
        x1 = _matmul(merged, wo_b, F32, tm=512, tn=1024, residual=xf, name="out_proj")
        hp, ids_slab, wts_slab = _norm_route(x1, norm_ffn[l], w_router_group[l], b_router_group[l],
                                             w_router_expert[l], b_router_expert[l])
        row_tok, row_w, pos, tile_e, tile_valid = _dispatch_plan(
            ids_slab[:, :TOP_K], wts_slab[:, :TOP_K], n_experts, moe_tm)
        hact = _expert_up(hp, row_tok, wg_b, wu_b, tile_e, tile_valid, moe_tm)
        ys = _expert_down(hact, wd_b, row_w, tile_e, tile_valid, moe_tm)
        last = l == depth - 1
        xf = _combine(x1, ys, pos, norm_final, final_norm=last)
    return xf.reshape(batch, seq, d)
```

```python
import functools
import math

import jax
import jax.numpy as jnp
from jax import lax
from jax.experimental import pallas as pl
from jax.experimental.pallas import tpu as pltpu

F32 = jnp.float32
BF16 = jnp.bfloat16
I32 = jnp.int32

HEAD_DIM = 128
POOL_SIZES = (2, 4, 8, 16)
POOL_HALO = 16
TOP_K = 2
RMS_EPS = 1e-6
LANES = 128
VMEM_LIMIT = 56 * 1024 * 1024
NEG_BIG = -1e30


def _cparams(sem):
    return pltpu.CompilerParams(dimension_semantics=sem, vmem_limit_bytes=VMEM_LIMIT)


def _tile(pref, dim):
    t = min(pref, dim)
    assert dim % t == 0, (pref, dim)
    return t


def _rmsnorm_kernel(x_ref, g_ref, o_ref):
    x = x_ref[...]
    ms = jnp.mean(x * x, axis=-1, keepdims=True)
    o_ref[...] = (x * lax.rsqrt(ms + RMS_EPS) * g_ref[...]).astype(o_ref.dtype)


def _rmsnorm(x, g, out_dtype):
    n, d = x.shape
    tm = _tile(512, n)
    return pl.pallas_call(
        _rmsnorm_kernel,
        grid=(n // tm,),
        in_specs=[pl.BlockSpec((tm, d), lambda i: (i, 0)),
                  pl.BlockSpec((1, d), lambda i: (0, 0))],
        out_specs=pl.BlockSpec((tm, d), lambda i: (i, 0)),
        out_shape=jax.ShapeDtypeStruct((n, d), out_dtype),
        compiler_params=_cparams(("parallel",)),
        name="rmsnorm",
    )(x, g.reshape(1, d))


def _mm_kernel(a_ref, b_ref, o_ref):
    o_ref[...] = jnp.dot(a_ref[...], b_ref[...], preferred_element_type=F32).astype(o_ref.dtype)


def _mm_res_kernel(a_ref, b_ref, r_ref, o_ref):
    o_ref[...] = (r_ref[...] + jnp.dot(a_ref[...], b_ref[...], preferred_element_type=F32)
                  ).astype(o_ref.dtype)


def _matmul(a, b, out_dtype, *, tm, tn, residual=None, name):
    m, k = a.shape
    _, n = b.shape
    tm, tn = _tile(tm, m), _tile(tn, n)
    in_specs = [pl.BlockSpec((tm, k), lambda i, j: (i, 0)),
                pl.BlockSpec((k, tn), lambda i, j: (0, j))]
    args = [a, b]
    body = _mm_kernel
    if residual is not None:
        in_specs.append(pl.BlockSpec((tm, tn), lambda i, j: (i, j)))
        args.append(residual)
        body = _mm_res_kernel
    return pl.pallas_call(
        body,
        grid=(m // tm, n // tn),
        in_specs=in_specs,
        out_specs=pl.BlockSpec((tm, tn), lambda i, j: (i, j)),
        out_shape=jax.ShapeDtypeStruct((m, n), out_dtype),
        compiler_params=_cparams(("parallel", "parallel")),
        name=name,
    )(*args)


ATTN_SUB = 256
ATTN_HEADS = 2


def _cast_stream_step(step, total_steps, srcs, dsts, inbufs, outbufs, in_sem, out_sem):
    slot = step % 2

    def rows(j):
        return inbufs[j].shape[1]

    def read(j, chunk, sl):
        start = pl.multiple_of(chunk * rows(j), rows(j))
        return pltpu.make_async_copy(srcs[j].at[pl.ds(start, rows(j))], inbufs[j].at[sl],
                                     in_sem.at[j, sl])

    def write(j, chunk, sl):
        start = pl.multiple_of(chunk * rows(j), rows(j))
        return pltpu.make_async_copy(outbufs[j].at[sl], dsts[j].at[pl.ds(start, rows(j))],
                                     out_sem.at[j, sl])

    by_chunks = {}
    for j in range(len(srcs)):
        by_chunks.setdefault(srcs[j].shape[0] // rows(j), []).append(j)

    for nchunks, js in by_chunks.items():
        assert 1 <= nchunks <= total_steps

        @pl.when(step == 0)
        def _(js=js):
            for j in js:
                read(j, 0, 0).start()

        @pl.when(step + 1 < nchunks)
        def _(js=js):
            for j in js:
                read(j, step + 1, 1 - slot).start()

        @pl.when((step >= 2) & (step < nchunks))
        def _(js=js):
            for j in js:
                write(j, step - 2, slot).wait()

        @pl.when(step < nchunks)
        def _(js=js):
            for j in js:
                read(j, step, slot).wait()
                outbufs[j][slot] = inbufs[j][slot].astype(outbufs[j].dtype)
                write(j, step, slot).start()

        @pl.when(step == nchunks - 1)
        def _(js=js, nchunks=nchunks):
            for j in js:
                write(j, step, slot).wait()
                if nchunks >= 2:
                    write(j, step - 1, 1 - slot).wait()


def _attn_kernel(q_ref, k_ref, v_ref, *rest, tq, sub, heads, n_stream, grid):
    ns = n_stream
    srcs, o_ref, dsts = rest[:ns], rest[ns], rest[ns + 1:2 * ns + 1]
    acc_ref, carry_ref = rest[2 * ns + 1:2 * ns + 3]
    inbufs, outbufs = rest[2 * ns + 3:3 * ns + 3], rest[3 * ns + 3:4 * ns + 3]
    _attn_tile(q_ref, k_ref, v_ref, o_ref, acc_ref, carry_ref, tq=tq, sub=sub, heads=heads)
    if ns:
        in_sem, out_sem = rest[4 * ns + 3:]
        step = (pl.program_id(0) * grid[1] + pl.program_id(1)) * grid[2] + pl.program_id(2)
        _cast_stream_step(step, grid[0] * grid[1] * grid[2], srcs, dsts, inbufs, outbufs,
                          in_sem, out_sem)


def _attn_tile(q_ref, k_ref, v_ref, o_ref, acc_ref, carry_ref, *, tq, sub, heads):
    i = pl.program_id(2)
    nsub = tq // sub
    qi = lax.broadcasted_iota(I32, (sub, sub), 0)
    ki = lax.broadcasted_iota(I32, (sub, sub), 1)
    tri = jnp.where(qi >= ki, 1.0, 0.0).astype(BF16)
    causal = ki < qi

    def attend(rows, start, nkeys, masked, carries, accs):
        nrows = rows.stop - rows.start
        start = pl.multiple_of(start, sub)
        hcols = [slice(h * HEAD_DIM, (h + 1) * HEAD_DIM) for h in range(heads)]
        z = [lax.dot_general(q_ref[rows, hc], k_ref[pl.ds(start, nkeys), hc],
                             (((1,), (1,)), ((), ())), preferred_element_type=F32) for hc in hcols]
        sp = [jnp.maximum(zh, 0.0) + jnp.log2(1.0 + jnp.exp2(-jnp.abs(zh))) for zh in z]
        if masked:
            sp = [jnp.where(causal, s, 0.0) for s in sp]
        chunks = [slice(c, c + sub) for c in range(0, nkeys, sub)]
        spb = [s.astype(BF16) for s in sp]
        cs = [[jnp.dot(s[:, ck], tri, preferred_element_type=F32) for ck in chunks] for s in spb]
        tot = [[jnp.broadcast_to(c[:, 0:1], (nrows, HEAD_DIM)) for c in ch] for ch in cs]
        new_carries, new_accs = [], []
        for h in range(heads):
            carry = carries[h]
            w = [None] * (nkeys // LANES)
            for cb in range(len(chunks) - 1, -1, -1):
                for c0 in range(0, sub, LANES):
                    g0 = cb * sub + c0
                    w[g0 // LANES] = jnp.exp2(z[h][:, g0:g0 + LANES] - cs[h][cb][:, c0:c0 + LANES]
                                              - carry)
                carry = carry + tot[h][cb]
            w = jnp.concatenate(w, axis=1)
            if masked:
                w = jnp.where(causal, w, 0.0)
            new_carries.append(carry)
            new_accs.append(accs[h] + jnp.dot(w.astype(BF16), v_ref[pl.ds(start, nkeys), hcols[h]],
                                              preferred_element_type=F32))
        return new_carries, new_accs

    base = i * tq
    for rb in range(nsub):
        rows = slice(rb * sub, (rb + 1) * sub)
        zero = [jnp.zeros((sub, HEAD_DIM), F32)] * heads
        carries, accs = attend(rows, base + rb * sub, sub, True, zero, zero)
        for cb in range(rb - 1, -1, -1):
            carries, accs = attend(rows, base + cb * sub, sub, False, carries, accs)
        acc_ref[rows, :] = jnp.concatenate(accs, axis=1)
        carry_ref[rows, :] = jnp.concatenate(carries, axis=1)

    def body(n, c):
        full = slice(0, tq)
        hcols = [slice(h * HEAD_DIM, (h + 1) * HEAD_DIM) for h in range(heads)]
        carries, accs = attend(full, (i - 1 - n) * tq, tq, False,
                               [carry_ref[:, hc] for hc in hcols], [acc_ref[:, hc] for hc in hcols])
        acc_ref[...] = jnp.concatenate(accs, axis=1)
        carry_ref[...] = jnp.concatenate(carries, axis=1)
        return c

    lax.fori_loop(0, i, body, 0)
    o_ref[...] = acc_ref[...].astype(o_ref.dtype)


BF16_SUBLANES = 16


def _attention(proj, batch, seq, n_heads, to_bf16=()):
    n = batch * seq
    tq = _tile(512, seq)
    sub = _tile(ATTN_SUB, tq)
    nq = seq // tq
    heads = _tile(ATTN_HEADS, n_heads)
    ng = n_heads // heads
    width = heads * HEAD_DIM
    grid = (batch, ng, nq)
    nsteps = batch * ng * nq
    ns = len(to_bf16)
    chunk_rows = [max(BF16_SUBLANES, a.shape[0] // nsteps) for a in to_bf16]
    for a, r in zip(to_bf16, chunk_rows):
        assert a.shape[0] % r == 0 and r % BF16_SUBLANES == 0 and a.shape[0] // r <= nsteps, (
            a.shape, nsteps)
    any_spec = pl.BlockSpec(memory_space=pl.ANY)
    kern = functools.partial(_attn_kernel, tq=tq, sub=sub, heads=heads, n_stream=ns, grid=grid)
    outs = pl.pallas_call(
        kern,
        grid=grid,
        in_specs=[pl.BlockSpec((tq, width), lambda b, g, i: (b * nq + i, g)),
                  pl.BlockSpec((seq, width), lambda b, g, i: (b, ng + g)),
                  pl.BlockSpec((seq, width), lambda b, g, i: (b, 2 * ng + g))] + [any_spec] * ns,
        out_specs=[pl.BlockSpec((tq, width), lambda b, g, i: (b * nq + i, g))] + [any_spec] * ns,
        out_shape=[jax.ShapeDtypeStruct((n, n_heads * HEAD_DIM), BF16)]
        + [jax.ShapeDtypeStruct(a.shape, BF16) for a in to_bf16],
        scratch_shapes=[pltpu.VMEM((tq, width), F32), pltpu.VMEM((tq, width), F32)]
        + [pltpu.VMEM((2, r, a.shape[1]), F32) for a, r in zip(to_bf16, chunk_rows)]
        + [pltpu.VMEM((2, r, a.shape[1]), BF16) for a, r in zip(to_bf16, chunk_rows)]
        + ([pltpu.SemaphoreType.DMA((ns, 2)), pltpu.SemaphoreType.DMA((ns, 2))] if ns else []),
        compiler_params=_cparams(("arbitrary", "arbitrary", "arbitrary")),
        name="stickbreak_attn",
    )(proj, proj, proj, *to_bf16)
    return outs[0], tuple(outs[1:])


def _pool_kernel(u_ref, halo_ref, w_ref, s_ref, o_ref, buf, *, ts, gd):
    i = pl.program_id(1)
    pos = i * ts + lax.broadcasted_iota(I32, (ts, gd), 0)
    for g, win in enumerate(POOL_SIZES):
        cols = slice(g * gd, (g + 1) * gd)
        u = u_ref[:, cols].astype(F32)
        halo = halo_ref[:, cols].astype(F32)
        buf[0:POOL_HALO, :] = jnp.where(i > 0, halo, 0.0)
        buf[POOL_HALO:POOL_HALO + ts, :] = u
        acc = u
        for k in range(1, win):
            acc = acc + buf[POOL_HALO - k:POOL_HALO - k + ts, :]
        cnt = jnp.minimum(pos + 1, win).astype(F32)
        pooled = acc / cnt - u
        mixed = jnp.dot(pooled.astype(BF16), w_ref[g], preferred_element_type=F32) * s_ref[g]
        o_ref[:, cols] = mixed.astype(o_ref.dtype)


def _pool_mix(proj, pool_w, pool_scale, batch, seq, col_block):
    n = batch * seq
    groups, gd, _ = pool_w.shape
    width = groups * gd
    ts = _tile(512, seq)
    ns = seq // ts
    hb = ts // POOL_HALO
    kern = functools.partial(_pool_kernel, ts=ts, gd=gd)
    return pl.pallas_call(
        kern,
        grid=(batch, ns),
        in_specs=[pl.BlockSpec((ts, width), lambda b, i: (b * ns + i, col_block)),
                  pl.BlockSpec((POOL_HALO, width),
                               lambda b, i: (jnp.maximum((b * ns + i) * hb - 1, 0), col_block)),
                  pl.BlockSpec((groups, gd, gd), lambda b, i: (0, 0, 0)),
                  pl.BlockSpec((groups, 1, gd), lambda b, i: (0, 0, 0))],
        out_specs=pl.BlockSpec((ts, width), lambda b, i: (b * ns + i, 0)),
        out_shape=jax.ShapeDtypeStruct((n, width), BF16),
        scratch_shapes=[pltpu.VMEM((POOL_HALO + ts, gd), F32)],
        compiler_params=_cparams(("parallel", "arbitrary")),
        name="pool_mix",
    )(proj, proj, pool_w, pool_scale.reshape(groups, 1, gd))


def _gated_kernel(oa_ref, pm_ref, wa_ref, wp_ref, ga_ref, gp_ref, o_ref):
    ya = jnp.dot(oa_ref[...], wa_ref[...], preferred_element_type=F32)
    yp = jnp.dot(pm_ref[...], wp_ref[...], preferred_element_type=F32)
    ga = jax.nn.sigmoid(ga_ref[...].astype(F32))
    gp = jax.nn.sigmoid(gp_ref[...].astype(F32))
    o_ref[...] = (ga * ya + gp * yp).astype(o_ref.dtype)


def _gated_merge(o_attn, p_mixed, w_a, w_p, proj, gate_col0):
    n, ka = o_attn.shape
    _, kp = p_mixed.shape
    d = w_a.shape[1]
    tm, tn = _tile(1024, n), _tile(512, d)
    gb = gate_col0 // tn
    nb = d // tn
    return pl.pallas_call(
        _gated_kernel,
        grid=(n // tm, nb),
        in_specs=[pl.BlockSpec((tm, ka), lambda i, j: (i, 0)),
                  pl.BlockSpec((tm, kp), lambda i, j: (i, 0)),
                  pl.BlockSpec((ka, tn), lambda i, j: (0, j)),
                  pl.BlockSpec((kp, tn), lambda i, j: (0, j)),
                  pl.BlockSpec((tm, tn), lambda i, j: (i, gb + j)),
                  pl.BlockSpec((tm, tn), lambda i, j: (i, gb + nb + j))],
        out_specs=pl.BlockSpec((tm, tn), lambda i, j: (i, j)),
        out_shape=jax.ShapeDtypeStruct((n, d), BF16),
        compiler_params=_cparams(("parallel", "parallel")),
        name="gated_merge",
    )(o_attn, p_mixed, w_a, w_p, proj, proj)


TOKEN_PAD_ROWS = 4


def _token_pitch(d):
    return d // LANES + TOKEN_PAD_ROWS


def _split2(x):
    hi = x.astype(BF16)
    lo = (x - hi.astype(F32)).astype(BF16)
    return hi, lo


def _norm_route_kernel(x_ref, g_ref, wr_ref, br_ref, hp_ref, ids_ref, wts_ref, *,
                       n_groups, per_group):
    x = x_ref[...]
    tm, d = x.shape
    ms = jnp.mean(x * x, axis=-1, keepdims=True)
    h = x * lax.rsqrt(ms + RMS_EPS) * g_ref[...]
    pitch = _token_pitch(d)
    for c in range(d // LANES):
        hp_ref[pl.ds(c, tm, stride=pitch), :] = h[:, c * LANES:(c + 1) * LANES]
    for c in range(d // LANES, pitch):
        hp_ref[pl.ds(c, tm, stride=pitch), :] = jnp.zeros((tm, LANES), F32)

    h2 = _split2(h)
    w2 = _split2(wr_ref[...])
    logits = br_ref[...] + jnp.zeros((tm, LANES), F32)
    for a in range(2):
        for b in range(2 - a):
            logits = logits + jnp.dot(h2[a], w2[b], preferred_element_type=F32)

    col = lax.broadcasted_iota(I32, (tm, LANES), 1).astype(F32)
    is_grp = col < n_groups
    gl = jnp.where(is_grp, logits, NEG_BIG)
    gmax = jnp.max(gl, axis=-1, keepdims=True)
    g_top = jnp.min(jnp.where(gl == gmax, col, float(LANES)), axis=-1, keepdims=True)
    denom = jnp.sum(jnp.where(is_grp, jnp.exp(gl - gmax), 0.0), axis=-1, keepdims=True)
    p_group = 1.0 / denom
    first = n_groups + g_top * per_group
    in_grp = (col >= first) & (col < first + per_group)
    el = jnp.where(in_grp, logits, NEG_BIG)
    v1 = jnp.max(el, axis=-1, keepdims=True)
    i1 = jnp.min(jnp.where(el == v1, col, float(LANES)), axis=-1, keepdims=True)
    el2 = jnp.where(col == i1, NEG_BIG, el)
    v2 = jnp.max(el2, axis=-1, keepdims=True)
    i2 = jnp.min(jnp.where(el2 == v2, col, float(LANES)), axis=-1, keepdims=True)
    e2 = jnp.exp(v2 - v1)
    w1 = p_group / (1.0 + e2)
    w2 = p_group * e2 / (1.0 + e2)
    ids = jnp.where(col == 0.0, i1 - n_groups, jnp.where(col == 1.0, i2 - n_groups, 0.0))
    ids_ref[...] = ids.astype(I32)
    wts_ref[...] = jnp.where(col == 0.0, w1, jnp.where(col == 1.0, w2, 0.0))


def _norm_route(x, g, w_rg, b_rg, w_re, b_re):
    n, d = x.shape
    n_groups = w_rg.shape[1]
    n_experts = w_re.shape[1]
    assert n_groups + n_experts <= LANES
    pad = LANES - n_groups - n_experts
    wr = jnp.concatenate([w_rg, w_re, jnp.zeros((d, pad), F32)], axis=1)
    br = jnp.concatenate([b_rg, b_re, jnp.zeros((pad,), F32)]).reshape(1, LANES)
    tm = _tile(256, n)
    kern = functools.partial(_norm_route_kernel, n_groups=n_groups,
                             per_group=n_experts // n_groups)
    return pl.pallas_call(
        kern,
        grid=(n // tm,),
        in_specs=[pl.BlockSpec((tm, d), lambda i: (i, 0)),
                  pl.BlockSpec((1, d), lambda i: (0, 0)),
                  pl.BlockSpec((d, LANES), lambda i: (0, 0)),
                  pl.BlockSpec((1, LANES), lambda i: (0, 0))],
        out_specs=[pl.BlockSpec((tm * _token_pitch(d), LANES), lambda i: (i, 0)),
                   pl.BlockSpec((tm, LANES), lambda i: (i, 0)),
                   pl.BlockSpec((tm, LANES), lambda i: (i, 0))],
        out_shape=[jax.ShapeDtypeStruct((n * _token_pitch(d), LANES), F32),
                   jax.ShapeDtypeStruct((n, LANES), I32),
                   jax.ShapeDtypeStruct((n, LANES), F32)],
        compiler_params=_cparams(("parallel",)),
        name="norm_route",
    )(x, g.reshape(1, d), wr, br)


def _start_token_copies(idx_ref, base, src_hbm, dst, sem, first, count, chunks, pitch):
    for r in range(first, first + count):
        row = idx_ref[base + r] * pitch
        pltpu.make_async_copy(src_hbm.at[pl.ds(row, chunks)], dst.at[pl.ds(r * pitch, chunks)],
                              sem).start()


def _expert_up_kernel(tok_ref, te_ref, valid_ref, h_hbm, wg_ref, wu_ref, o_ref, xbuf, xb, sem, *,
                      tm, fc):
    t = pl.program_id(0)
    last = pl.num_programs(0) - 1
    slot = t % 2
    valid = valid_ref[t] != 0
    f = o_ref.shape[1]
    nseg = 2 * (f // fc)
    per_seg = tm // nseg
    chunks = xb.shape[1] // LANES
    pitch = xbuf.shape[1] // tm

    @pl.when(t == 0)
    def _():
        _start_token_copies(tok_ref, 0, h_hbm, xbuf.at[0], sem.at[0], 0, tm, chunks, pitch)

    pltpu.make_async_copy(h_hbm.at[pl.ds(0, tm * chunks)], xbuf.at[slot, pl.ds(0, tm * chunks)],
                          sem.at[slot]).wait()

    def prefetch(seg):
        _start_token_copies(tok_ref, (t + 1) * tm, h_hbm, xbuf.at[1 - slot], sem.at[1 - slot],
                            seg * per_seg, per_seg, chunks, pitch)

    def ffn(more):
        for c in range(chunks):
            xb[:, c * LANES:(c + 1) * LANES] = xbuf[slot, pl.ds(c, tm, stride=pitch), :].astype(BF16)
        for ci, c in enumerate(range(0, f, fc)):
            if more:
                prefetch(2 * ci)
            g = jnp.dot(xb[...], wg_ref[0, :, c:c + fc], preferred_element_type=F32)
            if more:
                prefetch(2 * ci + 1)
            u = jnp.dot(xb[...], wu_ref[0, :, c:c + fc], preferred_element_type=F32)
            o_ref[:, c:c + fc] = (g * jax.nn.sigmoid(g) * u).astype(o_ref.dtype)

    def skip(more):
        if more:
            for seg in range(nseg):
                prefetch(seg)
        o_ref[...] = jnp.zeros_like(o_ref)

    for more, live in ((True, True), (True, False), (False, True), (False, False)):
        @pl.when(((t < last) == more) & (valid == live))
        def _(more=more, live=live):
            ffn(more) if live else skip(more)


def _expert_up(h, row_tok, w_gate, w_up, tile_e, tile_valid, tm):
    rows = row_tok.shape[0]
    _, d, f = w_gate.shape
    nt = rows // tm
    fc = _tile(256, f)
    return pl.pallas_call(
        functools.partial(_expert_up_kernel, tm=tm, fc=fc),
        grid_spec=pltpu.PrefetchScalarGridSpec(
            num_scalar_prefetch=3,
            grid=(nt,),
            in_specs=[pl.BlockSpec(memory_space=pl.ANY),
                      pl.BlockSpec((1, d, f), lambda t, tok, te, tv: (te[t], 0, 0)),
                      pl.BlockSpec((1, d, f), lambda t, tok, te, tv: (te[t], 0, 0))],
            out_specs=pl.BlockSpec((tm, f), lambda t, tok, te, tv: (t, 0)),
            scratch_shapes=[pltpu.VMEM((2, tm * _token_pitch(d), LANES), F32), pltpu.VMEM((tm, d), BF16),
                            pltpu.SemaphoreType.DMA((2,))]),
        out_shape=jax.ShapeDtypeStruct((rows, f), BF16),
        compiler_params=_cparams(("arbitrary",)),
        name="expert_up",
    )(row_tok, tile_e, tile_valid, h, w_gate, w_up)


def _expert_down_kernel(te_ref, valid_ref, h_ref, wd_ref, rw_ref, o_ref):
    t = pl.program_id(0)

    @pl.when(valid_ref[t] != 0)
    def _():
        y = jnp.dot(h_ref[...], wd_ref[0], preferred_element_type=F32)
        o_ref[...] = (y * rw_ref[...]).astype(o_ref.dtype)

    @pl.when(valid_ref[t] == 0)
    def _():
        o_ref[...] = jnp.zeros_like(o_ref)


def _expert_down(hact, w_down, row_w, tile_e, tile_valid, tm):
    rows, f = hact.shape
    _, _, d = w_down.shape
    nt = rows // tm
    return pl.pallas_call(
        _expert_down_kernel,
        grid_spec=pltpu.PrefetchScalarGridSpec(
            num_scalar_prefetch=2,
            grid=(nt,),
            in_specs=[pl.BlockSpec((tm, f), lambda t, te, tv: (t, 0)),
                      pl.BlockSpec((1, f, d), lambda t, te, tv: (te[t], 0, 0)),
                      pl.BlockSpec((tm, 1), lambda t, te, tv: (t, 0))],
            out_specs=pl.BlockSpec((tm, d), lambda t, te, tv: (t, 0))),
        out_shape=jax.ShapeDtypeStruct((rows, d), F32),
        compiler_params=_cparams(("arbitrary",)),
        name="expert_down",
    )(tile_e, tile_valid, hact, w_down, row_w.reshape(rows, 1))


COMBINE_ROWS = 8


def _combine_kernel(pos_ref, x_ref, ys_hbm, g_ref, o_ref, buf, sem, *, tm, final_norm):
    i = pl.program_id(0)
    last = pl.num_programs(0) - 1
    slot = i % 2

    def gather(tile, s):
        for k in range(TOP_K):
            for r in range(tm):
                p = pos_ref[TOP_K * (tile * tm + r) + k]
                pltpu.make_async_copy(ys_hbm.at[pl.ds(p, 1)], buf.at[s, k, pl.ds(r, 1)],
                                      sem.at[s, k]).start()

    @pl.when(i == 0)
    def _():
        gather(0, 0)

    for k in range(TOP_K):
        pltpu.make_async_copy(ys_hbm.at[pl.ds(0, tm)], buf.at[slot, k], sem.at[slot, k]).wait()

    def finish():
        for c in range(0, tm, COMBINE_ROWS):
            rows = slice(c, c + COMBINE_ROWS)
            y = x_ref[rows, :]
            for k in range(TOP_K):
                y = y + buf[slot, k, rows, :]
            if final_norm:
                ms = jnp.mean(y * y, axis=-1, keepdims=True)
                y = y * lax.rsqrt(ms + RMS_EPS) * g_ref[...]
            o_ref[rows, :] = y

    @pl.when(i < last)
    def _():
        gather(i + 1, 1 - slot)
        finish()

    @pl.when(i == last)
    def _():
        finish()


def _combine(x1, ys, pos, g, final_norm):
    n, d = x1.shape
    tm = _tile(256, n)
    return pl.pallas_call(
        functools.partial(_combine_kernel, tm=tm, final_norm=final_norm),
        grid_spec=pltpu.PrefetchScalarGridSpec(
            num_scalar_prefetch=1,
            grid=(n // tm,),
            in_specs=[pl.BlockSpec((tm, d), lambda i, pos: (i, 0)),
                      pl.BlockSpec(memory_space=pl.ANY),
                      pl.BlockSpec((1, d), lambda i, pos: (0, 0))],
            out_specs=pl.BlockSpec((tm, d), lambda i, pos: (i, 0)),
            scratch_shapes=[pltpu.VMEM((2, TOP_K, tm, d), F32),
                            pltpu.SemaphoreType.DMA((2, TOP_K))]),
        out_shape=jax.ShapeDtypeStruct((n, d), F32),
        compiler_params=_cparams(("arbitrary",)),
        name="moe_combine",
    )(pos, x1, ys, g.reshape(1, d))


def _dispatch_plan(expert_ids, weights, n_experts, tm):
    n = expert_ids.shape[0]
    a = n * TOP_K
    flat_e = expert_ids.reshape(a)
    onehot = (flat_e[:, None] == jnp.arange(n_experts, dtype=I32)[None, :]).astype(I32)
    incl = jnp.cumsum(onehot, axis=0)
    rank = jnp.sum((incl - onehot) * onehot, axis=1)
    counts = incl[-1]
    padded = (counts + tm - 1) // tm * tm
    pends = jnp.cumsum(padded)
    pstarts = pends - padded
    dest = (pstarts[flat_e] + rank).astype(I32)
    n_tiles = a // tm + n_experts
    rows = n_tiles * tm
    fields = jnp.stack([(jnp.arange(a, dtype=I32) // TOP_K).astype(F32), weights.reshape(a)], axis=1)
    row_fields = jnp.zeros((rows, 2), F32).at[dest].set(fields)
    row_tok = row_fields[:, 0].astype(I32)
    row_w = row_fields[:, 1]
    tile_start = jnp.arange(n_tiles, dtype=I32) * tm
    tile_e = jnp.minimum(jnp.sum((tile_start[:, None] >= pends[None, :]).astype(I32), axis=1),
                         n_experts - 1)
    tile_valid = (tile_start < pends[-1]).astype(I32)
    return row_tok, row_w, dest, tile_e, tile_valid


def kernel(x, norm_mix, w_in, w_o_attn, pool_w, pool_scale, w_o_pool, w_out, norm_ffn,
           w_router_group, b_router_group, w_router_expert, b_router_expert,
           w_gate, w_up, w_down, norm_final):
    batch, seq, d = x.shape
    n = batch * seq
    depth = norm_mix.shape[0]
    sb_width = w_o_attn.shape[1]
    n_heads = sb_width // HEAD_DIM
    groups, gd = pool_w.shape[1], pool_w.shape[2]
    pool_width = groups * gd
    n_experts = w_router_expert.shape[2]
    moe_tm = 256
    assert pool_width == sb_width and (3 * sb_width) % pool_width == 0

    xf = x.reshape(n, d)
    for l in range(depth):
        h = _rmsnorm(xf, norm_mix[l], BF16)
        col_scale = jnp.where(jnp.arange(w_in.shape[2]) < sb_width,
                              math.log2(math.e) / math.sqrt(HEAD_DIM), 1.0)
        w_in_l = (w_in[l] * col_scale[None, :].astype(F32)).astype(BF16)
        proj = _matmul(h, w_in_l, BF16, tm=1024, tn=1024, name="in_proj")
        d_expert = w_gate.shape[3]
        o_attn, (wg_b, wu_b, wd_b, wa_b, wp_b, wo_b) = _attention(
            proj, batch, seq, n_heads,
            to_bf16=(w_gate[l].reshape(n_experts * d, d_expert), w_up[l].reshape(n_experts * d, d_expert),
                     w_down[l].reshape(n_experts * d_expert, d), w_o_attn[l], w_o_pool[l], w_out[l]))
        wg_b = wg_b.reshape(n_experts, d, d_expert)
        wu_b = wu_b.reshape(n_experts, d, d_expert)
        wd_b = wd_b.reshape(n_experts, d_expert, d)
        p_mixed = _pool_mix(proj, pool_w[l].astype(BF16), pool_scale[l], batch, seq,
                            (3 * sb_width) // pool_width)
        merged = _gated_merge(o_attn, p_mixed, wa_b, wp_b, proj, 3 * sb_width + pool_width)
        x1 = _matmul(merged, wo_b, F32, tm=512, tn=1024, residual=xf, name="out_proj")
        hp, ids_slab, wts_slab = _norm_route(x1, norm_ffn[l], w_router_group[l], b_router_group[l],
                                             w_router_expert[l], b_router_expert[l])
        row_tok, row_w, pos, tile_e, tile_valid = _dispatch_plan(
            ids_slab[:, :TOP_K], wts_slab[:, :TOP_K], n_experts, moe_tm)
        hact = _expert_up(hp, row_tok, wg_b, wu_b, tile_e, tile_valid, moe_tm)
        ys = _expert_down(hact, wd_b, row_w, tile_e, tile_valid, moe_tm)
        last = l == depth - 1
        xf = _combine(x1, ys, pos, norm_final, final_norm=last)
    return xf.reshape(batch, seq, d)
```

```python
import functools
import math

import jax
import jax.numpy as jnp
from jax import lax
from jax.experimental import pallas as pl
from jax.experimental.pallas import tpu as pltpu

F32 = jnp.float32
BF16 = jnp.bfloat16
I32 = jnp.int32

HEAD_DIM = 128
POOL_SIZES = (2, 4, 8, 16)
POOL_HALO = 16
TOP_K = 2
RMS_EPS = 1e-6
LANES = 128
VMEM_LIMIT = 56 * 1024 * 1024
NEG_BIG = -1e30


def _cparams(sem):
    return pltpu.CompilerParams(dimension_semantics=sem, vmem_limit_bytes=VMEM_LIMIT)


def _tile(pref, dim):
    t = min(pref, dim)
    assert dim % t == 0, (pref, dim)
    return t


def _rmsnorm_kernel(x_ref, g_ref, o_ref):
    x = x_ref[...]
    ms = jnp.mean(x * x, axis=-1, keepdims=True)
    o_ref[...] = (x * lax.rsqrt(ms + RMS_EPS) * g_ref[...]).astype(o_ref.dtype)


def _rmsnorm(x, g, out_dtype):
    n, d = x.shape
    tm = _tile(512, n)
    return pl.pallas_call(
        _rmsnorm_kernel,
        grid=(n // tm,),
        in_specs=[pl.BlockSpec((tm, d), lambda i: (i, 0)),
                  pl.BlockSpec((1, d), lambda i: (0, 0))],
        out_specs=pl.BlockSpec((tm, d), lambda i: (i, 0)),
        out_shape=jax.ShapeDtypeStruct((n, d), out_dtype),
        compiler_params=_cparams(("parallel",)),
        name="rmsnorm",
    )(x, g.reshape(1, d))


def _mm_kernel(a_ref, b_ref, o_ref):
    o_ref[...] = jnp.dot(a_ref[...], b_ref[...], preferred_element_type=F32).astype(o_ref.dtype)


def _mm_res_kernel(a_ref, b_ref, r_ref, o_ref):
    o_ref[...] = (r_ref[...] + jnp.dot(a_ref[...], b_ref[...], preferred_element_type=F32)
                  ).astype(o_ref.dtype)


def _matmul(a, b, out_dtype, *, tm, tn, residual=None, name):
    m, k = a.shape
    _, n = b.shape
    tm, tn = _tile(tm, m), _tile(tn, n)
    in_specs = [pl.BlockSpec((tm, k), lambda i, j: (i, 0)),
                pl.BlockSpec((k, tn), lambda i, j: (0, j))]
    args = [a, b]
    body = _mm_kernel
    if residual is not None:
        in_specs.append(pl.BlockSpec((tm, tn), lambda i, j: (i, j)))
        args.append(residual)
        body = _mm_res_kernel
    return pl.pallas_call(
        body,
        grid=(m // tm, n // tn),
        in_specs=in_specs,
        out_specs=pl.BlockSpec((tm, tn), lambda i, j: (i, j)),
        out_shape=jax.ShapeDtypeStruct((m, n), out_dtype),
        compiler_params=_cparams(("parallel", "parallel")),
        name=name,
    )(*args)


ATTN_SUB = 256
ATTN_HEADS = 2


def _cast_stream_step(step, total_steps, srcs, dsts, inbufs, outbufs, in_sem, out_sem):
    slot = step % 2

    def rows(j):
        return inbufs[j].shape[1]

    def read(j, chunk, sl):
        start = pl.multiple_of(chunk * rows(j), rows(j))
        return pltpu.make_async_copy(srcs[j].at[pl.ds(start, rows(j))], inbufs[j].at[sl],
                                     in_sem.at[j, sl])

    def write(j, chunk, sl):
        start = pl.multiple_of(chunk * rows(j), rows(j))
        return pltpu.make_async_copy(outbufs[j].at[sl], dsts[j].at[pl.ds(start, rows(j))],
                                     out_sem.at[j, sl])

    by_chunks = {}
    for j in range(len(srcs)):
        by_chunks.setdefault(srcs[j].shape[0] // rows(j), []).append(j)

    for nchunks, js in by_chunks.items():
        assert 1 <= nchunks <= total_steps

        @pl.when(step == 0)
        def _(js=js):
            for j in js:
                read(j, 0, 0).start()

        @pl.when(step + 1 < nchunks)
        def _(js=js):
            for j in js:
                read(j, step + 1, 1 - slot).start()

        @pl.when((step >= 2) & (step < nchunks))
        def _(js=js):
            for j in js:
                write(j, step - 2, slot).wait()

        @pl.when(step < nchunks)
        def _(js=js):
            for j in js:
                read(j, step, slot).wait()
                outbufs[j][slot] = inbufs[j][slot].astype(outbufs[j].dtype)
                write(j, step, slot).start()

        @pl.when(step == nchunks - 1)
        def _(js=js, nchunks=nchunks):
            for j in js:
                write(j, step, slot).wait()
                if nchunks >= 2:
                    write(j, step - 1, 1 - slot).wait()


def _attn_kernel(q_ref, k_ref, v_ref, *rest, tq, sub, heads, n_stream, grid):
    ns = n_stream
    srcs, o_ref, dsts = rest[:ns], rest[ns], rest[ns + 1:2 * ns + 1]
    acc_ref, carry_ref = rest[2 * ns + 1:2 * ns + 3]
    inbufs, outbufs = rest[2 * ns + 3:3 * ns + 3], rest[3 * ns + 3:4 * ns + 3]
    _attn_tile(q_ref, k_ref, v_ref, o_ref, acc_ref, carry_ref, tq=tq, sub=sub, heads=heads)
    if ns:
        in_sem, out_sem = rest[4 * ns + 3:]
        step = (pl.program_id(0) * grid[1] + pl.program_id(1)) * grid[2] + pl.program_id(2)
        _cast_stream_step(step, grid[0] * grid[1] * grid[2], srcs, dsts, inbufs, outbufs,
                          in_sem, out_sem)


def _attn_tile(q_ref, k_ref, v_ref, o_ref, acc_ref, carry_ref, *, tq, sub, heads):
    i = pl.program_id(2)
    nsub = tq // sub
    qi = lax.broadcasted_iota(I32, (sub, sub), 0)
    ki = lax.broadcasted_iota(I32, (sub, sub), 1)
    tri = jnp.where(qi >= ki, 1.0, 0.0).astype(BF16)
    causal = ki < qi

    def attend(rows, start, nkeys, masked, carries, accs):
        nrows = rows.stop - rows.start
        start = pl.multiple_of(start, sub)
        hcols = [slice(h * HEAD_DIM, (h + 1) * HEAD_DIM) for h in range(heads)]
        z = [lax.dot_general(q_ref[rows, hc], k_ref[pl.ds(start, nkeys), hc],
                             (((1,), (1,)), ((), ())), preferred_element_type=F32) for hc in hcols]
        sp = [jnp.maximum(zh, 0.0) + jnp.log2(1.0 + jnp.exp2(-jnp.abs(zh))) for zh in z]
        if masked:
            sp = [jnp.where(causal, s, 0.0) for s in sp]
        chunks = [slice(c, c + sub) for c in range(0, nkeys, sub)]
        spb = [s.astype(BF16) for s in sp]
        cs = [[jnp.dot(s[:, ck], tri, preferred_element_type=F32) for ck in chunks] for s in spb]
        tot = [[jnp.broadcast_to(c[:, 0:1], (nrows, HEAD_DIM)) for c in ch] for ch in cs]
        new_carries, new_accs = [], []
        for h in range(heads):
            carry = carries[h]
            w = [None] * (nkeys // LANES)
            for cb in range(len(chunks) - 1, -1, -1):
                for c0 in range(0, sub, LANES):
                    g0 = cb * sub + c0
                    w[g0 // LANES] = jnp.exp2(z[h][:, g0:g0 + LANES] - cs[h][cb][:, c0:c0 + LANES]
                                              - carry)
                carry = carry + tot[h][cb]
            w = jnp.concatenate(w, axis=1)
            if masked:
                w = jnp.where(causal, w, 0.0)
            new_carries.append(carry)
            new_accs.append(accs[h] + jnp.dot(w.astype(BF16), v_ref[pl.ds(start, nkeys), hcols[h]],
                                              preferred_element_type=F32))
        return new_carries, new_accs

    base = i * tq
    for rb in range(nsub):
        rows = slice(rb * sub, (rb + 1) * sub)
        zero = [jnp.zeros((sub, HEAD_DIM), F32)] * heads
        carries, accs = attend(rows, base + rb * sub, sub, True, zero, zero)
        for cb in range(rb - 1, -1, -1):
            carries, accs = attend(rows, base + cb * sub, sub, False, carries, accs)
        acc_ref[rows, :] = jnp.concatenate(accs, axis=1)
        carry_ref[rows, :] = jnp.concatenate(carries, axis=1)

    def body(n, c):
        full = slice(0, tq)
        hcols = [slice(h * HEAD_DIM, (h + 1) * HEAD_DIM) for h in range(heads)]
        carries, accs = attend(full, (i - 1 - n) * tq, tq, False,
                               [carry_ref[:, hc] for hc in hcols], [acc_ref[:, hc] for hc in hcols])
        acc_ref[...] = jnp.concatenate(accs, axis=1)
        carry_ref[...] = jnp.concatenate(carries, axis=1)
        return c

    lax.fori_loop(0, i, body, 0)
    o_ref[...] = acc_ref[...].astype(o_ref.dtype)


BF16_SUBLANES = 16


def _attention(proj, batch, seq, n_heads, to_bf16=()):
    n = batch * seq
    tq = _tile(512, seq)
    sub = _tile(ATTN_SUB, tq)
    nq = seq // tq
    heads = _tile(ATTN_HEADS, n_heads)
    ng = n_heads // heads
    width = heads * HEAD_DIM
    grid = (batch, ng, nq)
    nsteps = batch * ng * nq
    ns = len(to_bf16)
    chunk_rows = [max(BF16_SUBLANES, a.shape[0] // nsteps) for a in to_bf16]
    for a, r in zip(to_bf16, chunk_rows):
        assert a.shape[0] % r == 0 and r % BF16_SUBLANES == 0 and a.shape[0] // r <= nsteps, (
            a.shape, nsteps)
    any_spec = pl.BlockSpec(memory_space=pl.ANY)
    kern = functools.partial(_attn_kernel, tq=tq, sub=sub, heads=heads, n_stream=ns, grid=grid)
    outs = pl.pallas_call(
        kern,
        grid=grid,
        in_specs=[pl.BlockSpec((tq, width), lambda b, g, i: (b * nq + i, g)),
                  pl.BlockSpec((seq, width), lambda b, g, i: (b, ng + g)),
                  pl.BlockSpec((seq, width), lambda b, g, i: (b, 2 * ng + g))] + [any_spec] * ns,
        out_specs=[pl.BlockSpec((tq, width), lambda b, g, i: (b * nq + i, g))] + [any_spec] * ns,
        out_shape=[jax.ShapeDtypeStruct((n, n_heads * HEAD_DIM), BF16)]
        + [jax.ShapeDtypeStruct(a.shape, BF16) for a in to_bf16],
        scratch_shapes=[pltpu.VMEM((tq, width), F32), pltpu.VMEM((tq, width), F32)]
        + [pltpu.VMEM((2, r, a.shape[1]), F32) for a, r in zip(to_bf16, chunk_rows)]
        + [pltpu.VMEM((2, r, a.shape[1]), BF16) for a, r in zip(to_bf16, chunk_rows)]
        + ([pltpu.SemaphoreType.DMA((ns, 2)), pltpu.SemaphoreType.DMA((ns, 2))] if ns else []),
        compiler_params=_cparams(("arbitrary", "arbitrary", "arbitrary")),
        name="stickbreak_attn",
    )(proj, proj, proj, *to_bf16)
    return outs[0], tuple(outs[1:])


def _pool_kernel(u_ref, halo_ref, w_ref, s_ref, o_ref, buf, *, ts, gd):
    i = pl.program_id(1)
    pos = i * ts + lax.broadcasted_iota(I32, (ts, gd), 0)
    for g, win in enumerate(POOL_SIZES):
        cols = slice(g * gd, (g + 1) * gd)
        u = u_ref[:, cols].astype(F32)
        halo = halo_ref[:, cols].astype(F32)
        buf[0:POOL_HALO, :] = jnp.where(i > 0, halo, 0.0)
        buf[POOL_HALO:POOL_HALO + ts, :] = u
        acc = u
        for k in range(1, win):
            acc = acc + buf[POOL_HALO - k:POOL_HALO - k + ts, :]
        cnt = jnp.minimum(pos + 1, win).astype(F32)
        pooled = acc / cnt - u
        mixed = jnp.dot(pooled.astype(BF16), w_ref[g], preferred_element_type=F32) * s_ref[g]
        o_ref[:, cols] = mixed.astype(o_ref.dtype)


def _pool_mix(proj, pool_w, pool_scale, batch, seq, col_block):
    n = batch * seq
    groups, gd, _ = pool_w.shape
    width = groups * gd
    ts = _tile(512, seq)
    ns = seq // ts
    hb = ts // POOL_HALO
    kern = functools.partial(_pool_kernel, ts=ts, gd=gd)
    return pl.pallas_call(
        kern,
        grid=(batch, ns),
        in_specs=[pl.BlockSpec((ts, width), lambda b, i: (b * ns + i, col_block)),
                  pl.BlockSpec((POOL_HALO, width),
                               lambda b, i: (jnp.maximum((b * ns + i) * hb - 1, 0), col_block)),
                  pl.BlockSpec((groups, gd, gd), lambda b, i: (0, 0, 0)),
                  pl.BlockSpec((groups, 1, gd), lambda b, i: (0, 0, 0))],
        out_specs=pl.BlockSpec((ts, width), lambda b, i: (b * ns + i, 0)),
        out_shape=jax.ShapeDtypeStruct((n, width), BF16),
        scratch_shapes=[pltpu.VMEM((POOL_HALO + ts, gd), F32)],
        compiler_params=_cparams(("parallel", "arbitrary")),
        name="pool_mix",
    )(proj, proj, pool_w, pool_scale.reshape(groups, 1, gd))


def _gated_kernel(oa_ref, pm_ref, wa_ref, wp_ref, ga_ref, gp_ref, o_ref):
    ya = jnp.dot(oa_ref[...], wa_ref[...], preferred_element_type=F32)
    yp = jnp.dot(pm_ref[...], wp_ref[...], preferred_element_type=F32)
    ga = jax.nn.sigmoid(ga_ref[...].astype(F32))
    gp = jax.nn.sigmoid(gp_ref[...].astype(F32))
    o_ref[...] = (ga * ya + gp * yp).astype(o_ref.dtype)


def _gated_merge(o_attn, p_mixed, w_a, w_p, proj, gate_col0):
    n, ka = o_attn.shape
    _, kp = p_mixed.shape
    d = w_a.shape[1]
    tm, tn = _tile(1024, n), _tile(512, d)
    gb = gate_col0 // tn
    nb = d // tn
    return pl.pallas_call(
        _gated_kernel,
        grid=(n // tm, nb),
        in_specs=[pl.BlockSpec((tm, ka), lambda i, j: (i, 0)),
                  pl.BlockSpec((tm, kp), lambda i, j: (i, 0)),
                  pl.BlockSpec((ka, tn), lambda i, j: (0, j)),
                  pl.BlockSpec((kp, tn), lambda i, j: (0, j)),
                  pl.BlockSpec((tm, tn), lambda i, j: (i, gb + j)),
                  pl.BlockSpec((tm, tn), lambda i, j: (i, gb + nb + j))],
        out_specs=pl.BlockSpec((tm, tn), lambda i, j: (i, j)),
        out_shape=jax.ShapeDtypeStruct((n, d), BF16),
        compiler_params=_cparams(("parallel", "parallel")),
        name="gated_merge",
    )(o_attn, p_mixed, w_a, w_p, proj, proj)


TOKEN_PAD_ROWS = 4


def _token_pitch(d):
    return d // LANES + TOKEN_PAD_ROWS


def _split2(x):
    hi = x.astype(BF16)
    lo = (x - hi.astype(F32)).astype(BF16)
    return hi, lo


def _norm_route_kernel(x_ref, g_ref, wr_ref, br_ref, hp_ref, ids_ref, wts_ref, *,
                       n_groups, per_group):
    x = x_ref[...]
    tm, d = x.shape
    ms = jnp.mean(x * x, axis=-1, keepdims=True)
    h = x * lax.rsqrt(ms + RMS_EPS) * g_ref[...]
    pitch = _token_pitch(d)
    for c in range(d // LANES):
        hp_ref[pl.ds(c, tm, stride=pitch), :] = h[:, c * LANES:(c + 1) * LANES]
    for c in range(d // LANES, pitch):
        hp_ref[pl.ds(c, tm, stride=pitch), :] = jnp.zeros((tm, LANES), F32)

    h2 = _split2(h)
    w2 = _split2(wr_ref[...])
    logits = br_ref[...] + jnp.zeros((tm, LANES), F32)
    for a in range(2):
        for b in range(2 - a):
            logits = logits + jnp.dot(h2[a], w2[b], preferred_element_type=F32)

    col = lax.broadcasted_iota(I32, (tm, LANES), 1).astype(F32)
    is_grp = col < n_groups
    gl = jnp.where(is_grp, logits, NEG_BIG)
    gmax = jnp.max(gl, axis=-1, keepdims=True)
    g_top = jnp.min(jnp.where(gl == gmax, col, float(LANES)), axis=-1, keepdims=True)
    denom = jnp.sum(jnp.where(is_grp, jnp.exp(gl - gmax), 0.0), axis=-1, keepdims=True)
    p_group = 1.0 / denom
    first = n_groups + g_top * per_group
    in_grp = (col >= first) & (col < first + per_group)
    el = jnp.where(in_grp, logits, NEG_BIG)
    v1 = jnp.max(el, axis=-1, keepdims=True)
    i1 = jnp.min(jnp.where(el == v1, col, float(LANES)), axis=-1, keepdims=True)
    el2 = jnp.where(col == i1, NEG_BIG, el)
    v2 = jnp.max(el2, axis=-1, keepdims=True)
    i2 = jnp.min(jnp.where(el2 == v2, col, float(LANES)), axis=-1, keepdims=True)
    e2 = jnp.exp(v2 - v1)
    w1 = p_group / (1.0 + e2)
    w2 = p_group * e2 / (1.0 + e2)
    ids = jnp.where(col == 0.0, i1 - n_groups, jnp.where(col == 1.0, i2 - n_groups, 0.0))
    ids_ref[...] = ids.astype(I32)
    wts_ref[...] = jnp.where(col == 0.0, w1, jnp.where(col == 1.0, w2, 0.0))


def _norm_route(x, g, w_rg, b_rg, w_re, b_re):
    n, d = x.shape
    n_groups = w_rg.shape[1]
    n_experts = w_re.shape[1]
    assert n_groups + n_experts <= LANES
    pad = LANES - n_groups - n_experts
    wr = jnp.concatenate([w_rg, w_re, jnp.zeros((d, pad), F32)], axis=1)
    br = jnp.concatenate([b_rg, b_re, jnp.zeros((pad,), F32)]).reshape(1, LANES)
    tm = _tile(256, n)
    kern = functools.partial(_norm_route_kernel, n_groups=n_groups,
                             per_group=n_experts // n_groups)
    return pl.pallas_call(
        kern,
        grid=(n // tm,),
        in_specs=[pl.BlockSpec((tm, d), lambda i: (i, 0)),
                  pl.BlockSpec((1, d), lambda i: (0, 0)),
                  pl.BlockSpec((d, LANES), lambda i: (0, 0)),
                  pl.BlockSpec((1, LANES), lambda i: (0, 0))],
        out_specs=[pl.BlockSpec((tm * _token_pitch(d), LANES), lambda i: (i, 0)),
                   pl.BlockSpec((tm, LANES), lambda i: (i, 0)),
                   pl.BlockSpec((tm, LANES), lambda i: (i, 0))],
        out_shape=[jax.ShapeDtypeStruct((n * _token_pitch(d), LANES), F32),
                   jax.ShapeDtypeStruct((n, LANES), I32),
                   jax.ShapeDtypeStruct((n, LANES), F32)],
        compiler_params=_cparams(("parallel",)),
        name="norm_route",
    )(x, g.reshape(1, d), wr, br)


def _start_token_copies(idx_ref, base, src_hbm, dst, sem, first, count, chunks, pitch):
    for r in range(first, first + count):
        row = idx_ref[base + r] * pitch
        pltpu.make_async_copy(src_hbm.at[pl.ds(row, chunks)], dst.at[pl.ds(r * pitch, chunks)],
                              sem).start()


def _expert_up_kernel(tok_ref, te_ref, valid_ref, h_hbm, wg_ref, wu_ref, o_ref, xbuf, xb, sem, *,
                      tm, fc):
    t = pl.program_id(0)
    last = pl.num_programs(0) - 1
    slot = t % 2
    valid = valid_ref[t] != 0
    f = o_ref.shape[1]
    nseg = 2 * (f // fc)
    per_seg = tm // nseg
    chunks = xb.shape[1] // LANES
    pitch = xbuf.shape[1] // tm

    @pl.when(t == 0)
    def _():
        _start_token_copies(tok_ref, 0, h_hbm, xbuf.at[0], sem.at[0], 0, tm, chunks, pitch)

    pltpu.make_async_copy(h_hbm.at[pl.ds(0, tm * chunks)], xbuf.at[slot, pl.ds(0, tm * chunks)],
                          sem.at[slot]).wait()

    def prefetch(seg):
        _start_token_copies(tok_ref, (t + 1) * tm, h_hbm, xbuf.at[1 - slot], sem.at[1 - slot],
                            seg * per_seg, per_seg, chunks, pitch)

    def ffn(more):
        for c in range(chunks):
            xb[:, c * LANES:(c + 1) * LANES] = xbuf[slot, pl.ds(c, tm, stride=pitch), :].astype(BF16)
        for ci, c in enumerate(range(0, f, fc)):
            if more:
                prefetch(2 * ci)
            g = jnp.dot(xb[...], wg_ref[0, :, c:c + fc], preferred_element_type=F32)
            if more:
                prefetch(2 * ci + 1)
            u = jnp.dot(xb[...], wu_ref[0, :, c:c + fc], preferred_element_type=F32)
            o_ref[:, c:c + fc] = (g * jax.nn.sigmoid(g) * u).astype(o_ref.dtype)

    def skip(more):
        if more:
            for seg in range(nseg):
                prefetch(seg)
        o_ref[...] = jnp.zeros_like(o_ref)

    for more, live in ((True, True), (True, False), (False, True), (False, False)):
        @pl.when(((t < last) == more) & (valid == live))
        def _(more=more, live=live):
            ffn(more) if live else skip(more)


def _expert_up(h, row_tok, w_gate, w_up, tile_e, tile_valid, tm):
    rows = row_tok.shape[0]
    _, d, f = w_gate.shape
    nt = rows // tm
    fc = _tile(256, f)
    return pl.pallas_call(
        functools.partial(_expert_up_kernel, tm=tm, fc=fc),
        grid_spec=pltpu.PrefetchScalarGridSpec(
            num_scalar_prefetch=3,
            grid=(nt,),
            in_specs=[pl.BlockSpec(memory_space=pl.ANY),
                      pl.BlockSpec((1, d, f), lambda t, tok, te, tv: (te[t], 0, 0)),
                      pl.BlockSpec((1, d, f), lambda t, tok, te, tv: (te[t], 0, 0))],
            out_specs=pl.BlockSpec((tm, f), lambda t, tok, te, tv: (t, 0)),
            scratch_shapes=[pltpu.VMEM((2, tm * _token_pitch(d), LANES), F32), pltpu.VMEM((tm, d), BF16),
                            pltpu.SemaphoreType.DMA((2,))]),
        out_shape=jax.ShapeDtypeStruct((rows, f), BF16),
        compiler_params=_cparams(("arbitrary",)),
        name="expert_up",
    )(row_tok, tile_e, tile_valid, h, w_gate, w_up)


def _expert_down_kernel(te_ref, valid_ref, h_ref, wd_ref, rw_ref, o_ref):
    t = pl.program_id(0)

    @pl.when(valid_ref[t] != 0)
    def _():
        y = jnp.dot(h_ref[...], wd_ref[0], preferred_element_type=F32)
        o_ref[...] = (y * rw_ref[...]).astype(o_ref.dtype)

    @pl.when(valid_ref[t] == 0)
    def _():
        o_ref[...] = jnp.zeros_like(o_ref)


def _expert_down(hact, w_down, row_w, tile_e, tile_valid, tm):
    rows, f = hact.shape
    _, _, d = w_down.shape
    nt = rows // tm
    return pl.pallas_call(
        _expert_down_kernel,
        grid_spec=pltpu.PrefetchScalarGridSpec(
            num_scalar_prefetch=2,
            grid=(nt,),
            in_specs=[pl.BlockSpec((tm, f), lambda t, te, tv: (t, 0)),
                      pl.BlockSpec((1, f, d), lambda t, te, tv: (te[t], 0, 0)),
                      pl.BlockSpec((tm, 1), lambda t, te, tv: (t, 0))],
            out_specs=pl.BlockSpec((tm, d), lambda t, te, tv: (t, 0))),
        out_shape=jax.ShapeDtypeStruct((rows, d), F32),
        compiler_params=_cparams(("arbitrary",)),
        name="expert_down",
    )(tile_e, tile_valid, hact, w_down, row_w.reshape(rows, 1))


COMBINE_ROWS = 8


def _combine_kernel(pos_ref, x_ref, ys_hbm, g_ref, o_ref, buf, sem, *, tm, final_norm):
    i = pl.program_id(0)
    last = pl.num_programs(0) - 1
    slot = i % 2

    def gather(tile, s):
        for k in range(TOP_K):
            for r in range(tm):
                p = pos_ref[TOP_K * (tile * tm + r) + k]
                pltpu.make_async_copy(ys_hbm.at[pl.ds(p, 1)], buf.at[s, k, pl.ds(r, 1)],
                                      sem.at[s, k]).start()

    @pl.when(i == 0)
    def _():
        gather(0, 0)

    for k in range(TOP_K):
        pltpu.make_async_copy(ys_hbm.at[pl.ds(0, tm)], buf.at[slot, k], sem.at[slot, k]).wait()

    def finish():
        for c in range(0, tm, COMBINE_ROWS):
            rows = slice(c, c + COMBINE_ROWS)
            y = x_ref[rows, :]
            for k in range(TOP_K):
                y = y + buf[slot, k, rows, :]
            if final_norm:
                ms = jnp.mean(y * y, axis=-1, keepdims=True)
                y = y * lax.rsqrt(ms + RMS_EPS) * g_ref[...]
            o_ref[rows, :] = y

    @pl.when(i < last)
    def _():
        gather(i + 1, 1 - slot)
        finish()

    @pl.when(i == last)
    def _():
        finish()


def _combine(x1, ys, pos, g, final_norm):
    n, d = x1.shape
    tm = _tile(256, n)
    return pl.pallas_call(
        functools.partial(_combine_kernel, tm=tm, final_norm=final_norm),
        grid_spec=pltpu.PrefetchScalarGridSpec(
            num_scalar_prefetch=1,
            grid=(n // tm,),
            in_specs=[pl.BlockSpec((tm, d), lambda i, pos: (i, 0)),
                      pl.BlockSpec(memory_space=pl.ANY),
                      pl.BlockSpec((1, d), lambda i, pos: (0, 0))],
            out_specs=pl.BlockSpec((tm, d), lambda i, pos: (i, 0)),
            scratch_shapes=[pltpu.VMEM((2, TOP_K, tm, d), F32),
                            pltpu.SemaphoreType.DMA((2, TOP_K))]),
        out_shape=jax.ShapeDtypeStruct((n, d), F32),
        compiler_params=_cparams(("arbitrary",)),
        name="moe_combine",
    )(pos, x1, ys, g.reshape(1, d))


def _dispatch_plan(expert_ids, weights, n_experts, tm):
    n = expert_ids.shape[0]
    a = n * TOP_K
    flat_e = expert_ids.reshape(a)
    onehot = (flat_e[:, None] == jnp.arange(n_experts, dtype=I32)[None, :]).astype(I32)
    incl = jnp.cumsum(onehot, axis=0)
    rank = jnp.sum((incl - onehot) * onehot, axis=1)
    counts = incl[-1]
    padded = (counts + tm - 1) // tm * tm
    pends = jnp.cumsum(padded)
    pstarts = pends - padded
    dest = (pstarts[flat_e] + rank).astype(I32)
    n_tiles = a // tm + n_experts
    rows = n_tiles * tm
    fields = jnp.stack([(jnp.arange(a, dtype=I32) // TOP_K).astype(F32), weights.reshape(a)], axis=1)
    row_fields = jnp.zeros((rows, 2), F32).at[dest].set(fields)
    row_tok = row_fields[:, 0].astype(I32)
    row_w = row_fields[:, 1]
    tile_start = jnp.arange(n_tiles, dtype=I32) * tm
    tile_e = jnp.minimum(jnp.sum((tile_start[:, None] >= pends[None, :]).astype(I32), axis=1),
                         n_experts - 1)
    tile_valid = (tile_start < pends[-1]).astype(I32)
    return row_tok, row_w, dest, tile_e, tile_valid


def kernel(x, norm_mix, w_in, w_o_attn, pool_w, pool_scale, w_o_pool, w_out, norm_ffn,
           w_router_group, b_router_group, w_router_expert, b_router_expert,
           w_gate, w_up, w_down, norm_final):
    batch, seq, d = x.shape
    n = batch * seq
    depth = norm_mix.shape[0]
    sb_width = w_o_attn.shape[1]
    n_heads = sb_width // HEAD_DIM
    groups, gd = pool_w.shape[1], pool_w.shape[2]
    pool_width = groups * gd
    n_experts = w_router_expert.shape[2]
    moe_tm = 256
    assert pool_width == sb_width and (3 * sb_width) % pool_width == 0

    xf = x.reshape(n, d)
    for l in range(depth):
        h = _rmsnorm(xf, norm_mix[l], BF16)
        col_scale = jnp.where(jnp.arange(w_in.shape[2]) < sb_width,
                              math.log2(math.e) / math.sqrt(HEAD_DIM), 1.0)
        w_in_l = (w_in[l] * col_scale[None, :].astype(F32)).astype(BF16)
        proj = _matmul(h, w_in_l, BF16, tm=1024, tn=1024, name="in_proj")
        d_expert = w_gate.shape[3]
        o_attn, (wg_b, wu_b, wd_b, wa_b, wp_b, wo_b) = _attention(
            proj, batch, seq, n_heads,
            to_bf16=(w_gate[l].reshape(n_experts * d, d_expert), w_up[l].reshape(n_experts * d, d_expert),
                     w_down[l].reshape(n_experts * d_expert, d), w_o_attn[l], w_o_pool[l], w_out[l]))
        wg_b = wg_b.reshape(n_experts, d, d_expert)
        wu_b = wu_b.reshape(n_experts, d, d_expert)
        wd_b = wd_b.reshape(n_experts, d_expert, d)
        p_mixed = _pool_mix(proj, pool_w[l].astype(BF16), pool_scale[l], batch, seq,
                            (3 * sb_width) // pool_width)
        merged = _gated_merge(o_attn, p_mixed, wa_b, wp_b, proj, 3 * sb_width + pool_width)
        x1 = _matmul(merged, wo_b, F32, tm=1024, tn=1024, residual=xf, name="out_proj")
        hp, ids_slab, wts_slab = _norm_route(x1, norm_ffn[l], w_router_group[l], b_router_group[l],
                                             w_router_expert[l], b_router_expert[l])
        row_tok, row_w, pos, tile_e, tile_valid = _dispatch_plan(
            ids_slab[:, :TOP_K], wts_slab[:, :TOP_K], n_experts, moe_tm)
        hact = _expert_up(hp, row_tok, wg_b, wu_b, tile_e, tile_valid, moe_tm)
        ys = _expert_down(hact, wd_b, row_w, tile_e, tile_valid, moe_tm)
        last = l == depth - 1
        xf = _combine(x1, ys, pos, norm_final, final_norm=last)
    return xf.reshape(batch, seq, d)
```
